```python
import math
import jax, jax.numpy as jnp
from jax import lax
import numpy as np

D_MODEL = 2048
BATCH = 4
SEQ = 2048
DEPTH = 4
DEC_BATCH = 128
DEC_SEQ = 4
PAST_LEN = 16384
PAGE_SIZE = 128

N_EVEN = (DEPTH + 1) // 2
N_ODD = DEPTH // 2

RW_HEAD = 64
RW_WIDTH = D_MODEL // 2
RW_HEADS = RW_WIDTH // RW_HEAD
RW_LORA_W = 64
RW_LORA_A = 64
RW_LORA_V = 32
RW_LORA_G = 160
RW_PROJ = 3 * RW_WIDTH + RW_LORA_W + RW_LORA_A + RW_LORA_G
RW_GN_EPS = 64e-5
SSM_WIDTH = D_MODEL
SSM_HEAD = 64
SSM_HEADS = SSM_WIDTH // SSM_HEAD
SSM_STATE = 128
SSM_GROUPS = 4
SSM_CONV = 4
SSM_CONV_CH = SSM_WIDTH + 2 * SSM_GROUPS * SSM_STATE
SSM_PROJ = SSM_WIDTH + SSM_CONV_CH + SSM_HEADS
SSM_CHUNK = 128
EVEN_PROJ = RW_PROJ + SSM_PROJ
EVEN_MIX = RW_WIDTH + SSM_WIDTH
HG_WIDTH = D_MODEL
HG_EXPAND = 128
HG_HEADS = HG_WIDTH // HG_EXPAND
HG_VHEAD = HG_WIDTH // HG_HEADS
HG_CHUNK = 64
ODD_PROJ = 4 * HG_WIDTH
MOE_GROUPS = 4
MOE_PER_GROUP = 8
MOE_EXPERTS = MOE_GROUPS * MOE_PER_GROUP
MOE_TOPK = 2
MOE_HIDDEN = D_MODEL // 4
MOE_BLOCK = 64
DN_ALPHA = (2 * DEPTH) ** 0.25
DN_BETA = (8 * DEPTH) ** -0.25
NORM_EPS = 1e-5

kernel_name = 'rwkv7_mamba2_hgrn2_hmoe_step'


def layer_norm(x, g, b):
    xf = x.astype(jnp.float32)
    mu = jnp.mean(xf, -1, keepdims=True)
    var = jnp.mean(jnp.square(xf - mu), -1, keepdims=True)
    return ((xf - mu) * lax.rsqrt(var + NORM_EPS) * g + b).astype(x.dtype)


def group_rms_norm(x, w, n_groups):
    shp = x.shape
    xg = x.reshape(shp[:-1] + (n_groups, shp[-1] // n_groups))
    xg = xg * lax.rsqrt(jnp.mean(jnp.square(xg), -1, keepdims=True) + NORM_EPS)
    return xg.reshape(shp) * w


def head_group_norm(o, g, b):
    mu = jnp.mean(o, -1, keepdims=True)
    var = jnp.mean(jnp.square(o - mu), -1, keepdims=True)
    on = (o - mu) * lax.rsqrt(var + RW_GN_EPS)
    return on * g.reshape(RW_HEADS, RW_HEAD) + b.reshape(RW_HEADS, RW_HEAD)


def token_shift(p, prev, mu):
    p_prev = jnp.concatenate([prev[:, None, :].astype(p.dtype), p[:, :-1]], axis=1)
    return p + (p_prev - p) * mu, p[:, -1]


def causal_depthwise_conv(u, buf, w, b):
    ch = u.shape[-1]
    full = jnp.concatenate([buf.astype(u.dtype), u], axis=1)
    y = lax.conv_general_dilated(full, w.astype(u.dtype)[:, None, :], (1,), 'VALID',
                                 dimension_numbers=('NWC', 'WIO', 'NWC'), feature_group_count=ch)
    return y + b.astype(u.dtype), full[:, full.shape[1] - (SSM_CONV - 1):]


def to_chunks(t, q):
    b, l = t.shape[:2]
    lp = -(-l // q) * q
    t = jnp.pad(t, [(0, 0), (0, lp - l)] + [(0, 0)] * (t.ndim - 2))
    return jnp.moveaxis(t.reshape((b, lp // q, q) + t.shape[2:]), 1, 0)


def from_chunks(y, l):
    y = jnp.moveaxis(y, 0, 1)
    return y.reshape((y.shape[0], -1) + y.shape[3:])[:, :l]


def wkv7_scan(s0, r, w, k, v, kk, a):
    def step(s, inp):
        r_t, w_t, k_t, v_t, kk_t, a_t = inp
        sa = jnp.einsum('bhvk,bhk->bhv', s, -kk_t)
        s = (s * w_t[:, :, None, :] + sa[..., None] * (kk_t * a_t)[:, :, None, :]
             + v_t[..., None] * k_t[:, :, None, :])
        return s, jnp.einsum('bhvk,bhk->bhv', s, r_t)
    xs = tuple(jnp.moveaxis(t, 1, 0) for t in (r, w, k, v, kk, a))
    s, ys = lax.scan(step, s0, xs)
    return jnp.moveaxis(ys, 0, 1), s


def ssd_chunked(xs, dt, a_head, bm, cm, s0):
    l = xs.shape[1]
    q = min(SSM_CHUNK, l)
    hpg = SSM_HEADS // SSM_GROUPS
    tri = jnp.tril(jnp.ones((q, q), bool))

    def step(s, inp):
        xc, dtc, bc, cc = inp
        cum = jnp.cumsum(dtc * a_head, axis=1)
        seg = cum[:, :, None, :] - cum[:, None, :, :]
        lmat = jnp.exp(jnp.where(tri[None, :, :, None], seg, -jnp.inf))
        cb = jnp.repeat(jnp.einsum('btgn,bsgn->btsg', cc, bc), hpg, axis=-1)
        xdt = xc * dtc[..., None]
        y = jnp.einsum('btsh,bshp->bthp', cb * lmat, xdt)
        ch = jnp.repeat(cc, hpg, axis=2)
        bh = jnp.repeat(bc, hpg, axis=2)
        y = y + jnp.einsum('bthn,bhpn->bthp', ch, s) * jnp.exp(cum)[..., None]
        to_end = jnp.exp(cum[:, -1:, :] - cum)
        s = (s * jnp.exp(cum[:, -1])[:, :, None, None]
             + jnp.einsum('bshn,bshp->bhpn', bh * to_end[..., None], xdt))
        return s, y

    s, ys = lax.scan(step, s0, (to_chunks(xs, q), to_chunks(dt, q), to_chunks(bm, q), to_chunks(cm, q)))
    return from_chunks(ys, l), s


def gla_chunked(qh, kh, vh, logf, s0):
    l = qh.shape[1]
    q = min(HG_CHUNK, l)
    tri = jnp.tril(jnp.ones((q, q), bool))

    def step(s, inp):
        qc, kc, vc, fc = inp
        cum = jnp.cumsum(fc, axis=1)
        dec = jnp.exp(jnp.where(tri[None, :, :, None, None], cum[:, :, None] - cum[:, None], -jnp.inf))
        att = jnp.einsum('bthk,bshk,btshk->bths', qc, kc, dec)
        y = jnp.einsum('bths,bshv->bthv', att, vc)
        y = y + jnp.einsum('bthk,bhkv->bthv', qc * jnp.exp(cum), s)
        s = (s * jnp.exp(cum[:, -1])[..., None]
             + jnp.einsum('bshk,bshv->bhkv', kc * jnp.exp(cum[:, -1:] - cum), vc))
        return s, y

    s, ys = lax.scan(step, s0, (to_chunks(qh, q), to_chunks(kh, q), to_chunks(vh, q), to_chunks(logf, q)))
    return from_chunks(ys, l), s


def even_mixer(x, shift0, wkv0, conv0, ssm0, v_first, vres, w_in, w_out, mu, w0, w_up, a0, a_up, g_up,
               k_k, k_a, r_k, lnx_g, lnx_b, conv_w, conv_b, dt_bias, a_log, d_skip, norm_w):
    b, l, _ = x.shape
    p = x @ w_in
    p_rw, p_ssm = p[..., :RW_PROJ], p[..., RW_PROJ:]
    f32 = jnp.float32
    pm, new_shift = token_shift(p_rw, shift0, mu)
    pm = pm.astype(f32)
    cuts = [RW_WIDTH, 2 * RW_WIDTH, 3 * RW_WIDTH, 3 * RW_WIDTH + RW_LORA_W, 3 * RW_WIDTH + RW_LORA_W + RW_LORA_A]
    r, k, v, wd, ad, gd = jnp.split(pm, cuts, axis=-1)
    log_w = -jax.nn.softplus(-(w0 + jnp.tanh(wd) @ w_up)) - 0.5
    decay = jnp.exp(-jnp.exp(log_w))
    if vres is None:
        v_first = v
    else:
        v0, v_down, v_up = vres
        v = v + (v_first - v) * jax.nn.sigmoid(v0 + (v @ v_down) @ v_up)
    a = jax.nn.sigmoid(a0 + ad @ a_up)
    g = jax.nn.sigmoid(gd) @ g_up
    heads = lambda t: t.reshape(b, l, RW_HEADS, RW_HEAD)
    kk = heads(k * k_k)
    kk = kk * lax.rsqrt(jnp.maximum(jnp.sum(jnp.square(kk), -1, keepdims=True), 1e-24))
    k = k * (1.0 + (a - 1.0) * k_a)
    rh, kh, vh, ah = heads(r), heads(k), heads(v), heads(a)
    o, new_wkv = wkv7_scan(wkv0.astype(f32), rh, heads(decay), kh, vh, kk, ah)
    o = head_group_norm(o, lnx_g, lnx_b)
    o = o + jnp.sum(rh * kh * r_k, -1, keepdims=True) * vh
    o_rw = o.reshape(b, l, RW_WIDTH) * g
    z = p_ssm[..., :SSM_WIDTH]
    xbc = p_ssm[..., SSM_WIDTH:SSM_WIDTH + SSM_CONV_CH]
    dt_raw = p_ssm[..., SSM_WIDTH + SSM_CONV_CH:]
    xbc, new_conv = causal_depthwise_conv(xbc, conv0, conv_w, conv_b)
    xbc = jax.nn.silu(xbc).astype(f32)
    gn = SSM_GROUPS * SSM_STATE
    xs = xbc[..., :SSM_WIDTH].reshape(b, l, SSM_HEADS, SSM_HEAD)
    bm = xbc[..., SSM_WIDTH:SSM_WIDTH + gn].reshape(b, l, SSM_GROUPS, SSM_STATE)
    cm = xbc[..., SSM_WIDTH + gn:].reshape(b, l, SSM_GROUPS, SSM_STATE)
    dt = jax.nn.softplus(dt_raw.astype(f32) + dt_bias)
    a_head = -jnp.exp(a_log.astype(f32))
    y, new_ssm = ssd_chunked(xs, dt, a_head, bm, cm, ssm0.astype(f32))
    y = y + d_skip[:, None] * xs
    y = group_rms_norm(y.reshape(b, l, SSM_WIDTH) * jax.nn.silu(z.astype(f32)), norm_w, SSM_GROUPS)
    mix = jnp.concatenate([o_rw, y], axis=-1).astype(x.dtype) @ w_out
    return (mix, v_first, new_shift.astype(shift0.dtype), new_wkv.astype(wkv0.dtype),
            new_conv.astype(conv0.dtype), new_ssm.astype(ssm0.dtype))


def hgrn2_mixer(x, s0, w_in, w_out, lower, norm_w):
    b, l, _ = x.shape
    f32 = jnp.float32
    q, f, i, g = jnp.split(x @ w_in, 4, axis=-1)
    fg = lower + (1.0 - lower) * jax.nn.sigmoid(f.astype(f32))
    heads = lambda t: t.reshape(b, l, HG_HEADS, -1)
    qh = heads(jax.nn.silu(q.astype(f32)))
    o, s = gla_chunked(qh, heads(1.0 - fg), heads(i.astype(f32)), heads(jnp.log(fg)), s0.astype(f32))
    o = group_rms_norm(o.reshape(b, l, HG_WIDTH), norm_w, HG_HEADS) * jax.nn.silu(g.astype(f32))
    return o.astype(x.dtype) @ w_out, s.astype(s0.dtype)


def hier_moe(x, wr_grp, br_grp, wr_exp, br_exp, w_gate, w_up, w_down):
    shp = x.shape
    d = shp[-1]
    xt = x.reshape(-1, d)
    t = xt.shape[0]
    f32 = jnp.float32
    grp_prob = jax.nn.softmax((xt @ wr_grp).astype(f32) + br_grp.astype(f32), axis=-1)
    grp_p, grp_i = lax.top_k(grp_prob, 1)
    exp_logit_all = jnp.einsum('td,gde->tge', xt, wr_exp).astype(f32) + br_exp.astype(f32)
    exp_logit = jnp.take_along_axis(exp_logit_all, grp_i[:, :, None], axis=1)[:, 0]
    top_p, top_i = lax.top_k(jax.nn.softmax(exp_logit, axis=-1), MOE_TOPK)
    gates = grp_p * top_p / jnp.sum(top_p, -1, keepdims=True)
    expert = grp_i * MOE_PER_GROUP + top_i
    m = t * MOE_TOPK
    flat_e = expert.reshape(m)
    flat_t = jnp.repeat(jnp.arange(t, dtype=jnp.int32), MOE_TOPK)
    flat_g = gates.reshape(m)
    order = jnp.argsort(flat_e, stable=True)
    sorted_e = flat_e[order]
    counts = jnp.bincount(flat_e, length=MOE_EXPERTS)
    starts = jnp.cumsum(counts) - counts
    padded = (counts + MOE_BLOCK - 1) // MOE_BLOCK * MOE_BLOCK
    pad_ends = jnp.cumsum(padded)
    slot = pad_ends[sorted_e] - padded[sorted_e] + jnp.arange(m, dtype=jnp.int32) - starts[sorted_e]
    n_blocks = -(-m // MOE_BLOCK) + MOE_EXPERTS
    n_slots = n_blocks * MOE_BLOCK
    slot_tok = jnp.full((n_slots,), t, jnp.int32).at[slot].set(flat_t[order])
    slot_gate = jnp.zeros((n_slots,), f32).at[slot].set(flat_g[order])
    block_e = jnp.minimum(jnp.searchsorted(pad_ends, jnp.arange(n_blocks, dtype=jnp.int32) * MOE_BLOCK,
                                           side='right'), MOE_EXPERTS - 1)
    x_rows = jnp.concatenate([xt, jnp.zeros((1, d), xt.dtype)], axis=0)[slot_tok]
    x_rows = x_rows.reshape(n_blocks, MOE_BLOCK, d)

    def expert_block(args):
        xb, e = args
        h = jax.nn.silu(xb @ w_gate[e]) * (xb @ w_up[e])
        return h @ w_down[e]

    ys = lax.map(expert_block, (x_rows, block_e)).reshape(n_slots, d)
    out = jnp.zeros((t + 1, d), ys.dtype).at[slot_tok].add(ys * slot_gate[:, None].astype(ys.dtype))
    return out[:t].reshape(shp)


def setup_inputs(seed: int = 0) -> dict:
    key = jax.random.key(seed)
    keys = jax.random.split(key, 64)
    counter = [0]

    def nk():
        counter[0] += 1
        return keys[counter[0] - 1]

    def nrm(shape, scale):
        return jax.random.normal(nk(), shape, jnp.float32) * scale

    def uni(shape, lo, hi):
        return jax.random.uniform(nk(), shape, jnp.float32, lo, hi)

    ne, no = N_EVEN, N_ODD
    dt = jnp.exp(uni((ne, SSM_HEADS), math.log(1e-3), math.log(1e-1)))
    return {
        'x_prompt': nrm((BATCH, SEQ, D_MODEL), 1.0),
        'x_sample': nrm((DEC_BATCH, DEC_SEQ, D_MODEL), 1.0),
        'state_rwkv_shift': nrm((ne, DEC_BATCH, RW_PROJ), 1.0),
        'state_rwkv_wkv': nrm((ne, DEC_BATCH, RW_HEADS, RW_HEAD, RW_HEAD), 0.3),
        'state_ssm_conv': nrm((ne, DEC_BATCH, SSM_CONV - 1, SSM_CONV_CH), 1.0),
        'state_ssm': nrm((ne, DEC_BATCH, SSM_HEADS, SSM_HEAD, SSM_STATE), 0.3),
        'state_hgrn': nrm((no, DEC_BATCH, HG_HEADS, HG_EXPAND, HG_VHEAD), 0.3),
        'ev_w_in': nrm((ne, D_MODEL, EVEN_PROJ), D_MODEL ** -0.5),
        'ev_w_out': nrm((ne, EVEN_MIX, D_MODEL), EVEN_MIX ** -0.5 * DN_BETA),
        'rw_mu': uni((ne, RW_PROJ), 0.0, 1.0),
        'rw_w0': uni((ne, RW_WIDTH), -6.0, 1.0),
        'rw_w_up': nrm((ne, RW_LORA_W, RW_WIDTH), 0.1),
        'rw_a0': nrm((ne, RW_WIDTH), 0.1),
        'rw_a_up': nrm((ne, RW_LORA_A, RW_WIDTH), RW_LORA_A ** -0.5),
        'rw_g_up': nrm((ne, RW_LORA_G, RW_WIDTH), RW_LORA_G ** -0.5),
        'rw_k_k': 0.85 + nrm((ne, RW_WIDTH), 0.02),
        'rw_k_a': 1.0 + nrm((ne, RW_WIDTH), 0.02),
        'rw_r_k': nrm((ne, RW_HEADS, RW_HEAD), 0.1),
        'rw_lnx_g': 1.0 + nrm((ne, RW_WIDTH), 0.02),
        'rw_lnx_b': nrm((ne, RW_WIDTH), 0.02),
        'rw_v0': 1.0 + nrm((ne - 1, RW_WIDTH), 0.1),
        'rw_v_down': nrm((ne - 1, RW_WIDTH, RW_LORA_V), RW_WIDTH ** -0.5),
        'rw_v_up': nrm((ne - 1, RW_LORA_V, RW_WIDTH), RW_LORA_V ** -0.5),
        'ssm_conv_w': nrm((ne, SSM_CONV, SSM_CONV_CH), SSM_CONV ** -0.5),
        'ssm_conv_b': nrm((ne, SSM_CONV_CH), 0.02),
        'ssm_dt_bias': dt + jnp.log(-jnp.expm1(-dt)),
        'ssm_a_log': jnp.log(uni((ne, SSM_HEADS), 1.0, 16.0)),
        'ssm_d': 1.0 + nrm((ne, SSM_HEADS), 0.1),
        'ssm_norm_w': 1.0 + nrm((ne, SSM_WIDTH), 0.02),
        'od_w_in': nrm((no, D_MODEL, ODD_PROJ), D_MODEL ** -0.5),
        'od_w_out': nrm((no, HG_WIDTH, D_MODEL), HG_WIDTH ** -0.5 * DN_BETA),
        'hg_lower_bounds': nrm((DEPTH, HG_WIDTH), 0.1),
        'hg_norm_w': 1.0 + nrm((no, HG_WIDTH), 0.02),
        'ln1_g': 1.0 + nrm((DEPTH, D_MODEL), 0.02),
        'ln1_b': nrm((DEPTH, D_MODEL), 0.02),
        'ln2_g': 1.0 + nrm((DEPTH, D_MODEL), 0.02),
        'ln2_b': nrm((DEPTH, D_MODEL), 0.02),
        'moe_wr_grp': nrm((DEPTH, D_MODEL, MOE_GROUPS), D_MODEL ** -0.5),
        'moe_br_grp': nrm((DEPTH, MOE_GROUPS), 0.01),
        'moe_wr_exp': nrm((DEPTH, MOE_GROUPS, D_MODEL, MOE_PER_GROUP), D_MODEL ** -0.5),
        'moe_br_exp': nrm((DEPTH, MOE_GROUPS, MOE_PER_GROUP), 0.01),
        'moe_w_gate': nrm((DEPTH, MOE_EXPERTS, D_MODEL, MOE_HIDDEN), D_MODEL ** -0.5),
        'moe_w_up': nrm((DEPTH, MOE_EXPERTS, D_MODEL, MOE_HIDDEN), D_MODEL ** -0.5),
        'moe_w_down': nrm((DEPTH, MOE_EXPERTS, MOE_HIDDEN, D_MODEL), MOE_HIDDEN ** -0.5 * DN_BETA),
    }


def reference(x_prompt, x_sample, state_rwkv_shift, state_rwkv_wkv, state_ssm_conv, state_ssm, state_hgrn,
              ev_w_in, ev_w_out, rw_mu, rw_w0, rw_w_up, rw_a0, rw_a_up, rw_g_up, rw_k_k, rw_k_a, rw_r_k,
              rw_lnx_g, rw_lnx_b, rw_v0, rw_v_down, rw_v_up, ssm_conv_w, ssm_conv_b, ssm_dt_bias, ssm_a_log,
              ssm_d, ssm_norm_w, od_w_in, od_w_out, hg_lower_bounds, hg_norm_w, ln1_g, ln1_b, ln2_g, ln2_b,
              moe_wr_grp, moe_br_grp, moe_wr_exp, moe_br_exp, moe_w_gate, moe_w_up, moe_w_down):
    lb_soft = jax.nn.softmax(hg_lower_bounds.astype(jnp.float32), axis=0)
    lower = jnp.cumsum(lb_soft, axis=0) - lb_soft[0]

    def run(x, shift, wkv, conv, ssm, hg):
        v_first = None
        n_shift, n_wkv, n_conv, n_ssm, n_hg = [], [], [], [], []
        for layer in range(DEPTH):
            if layer % 2 == 0:
                e = layer // 2
                vres = None if e == 0 else (rw_v0[e - 1], rw_v_down[e - 1], rw_v_up[e - 1])
                mix, v_first, s_sh, s_wkv, s_cv, s_ss = even_mixer(
                    x, shift[e], wkv[e], conv[e], ssm[e], v_first, vres, ev_w_in[e], ev_w_out[e], rw_mu[e],
                    rw_w0[e], rw_w_up[e], rw_a0[e], rw_a_up[e], rw_g_up[e], rw_k_k[e], rw_k_a[e], rw_r_k[e],
                    rw_lnx_g[e], rw_lnx_b[e], ssm_conv_w[e], ssm_conv_b[e], ssm_dt_bias[e], ssm_a_log[e],
                    ssm_d[e], ssm_norm_w[e])
                n_shift.append(s_sh)
                n_wkv.append(s_wkv)
                n_conv.append(s_cv)
                n_ssm.append(s_ss)
            else:
                o = layer // 2
                mix, s_hg = hgrn2_mixer(x, hg[o], od_w_in[o], od_w_out[o], lower[layer], hg_norm_w[o])
                n_hg.append(s_hg)
            x = layer_norm(DN_ALPHA * x + mix, ln1_g[layer], ln1_b[layer])
            ffn = hier_moe(x, moe_wr_grp[layer], moe_br_grp[layer], moe_wr_exp[layer], moe_br_exp[layer],
                           moe_w_gate[layer], moe_w_up[layer], moe_w_down[layer])
            x = layer_norm(DN_ALPHA * x + ffn, ln2_g[layer], ln2_b[layer])
        return (x, jnp.stack(n_shift), jnp.stack(n_wkv), jnp.stack(n_conv), jnp.stack(n_ssm),
                jnp.stack(n_hg))

    bp = x_prompt.shape[0]
    zeros_like_rows = lambda s: jnp.zeros((s.shape[0], bp) + s.shape[2:], s.dtype)
    y_prompt, p_shift, p_wkv, p_conv, p_ssm, p_hg = run(
        x_prompt, zeros_like_rows(state_rwkv_shift), zeros_like_rows(state_rwkv_wkv),
        zeros_like_rows(state_ssm_conv), zeros_like_rows(state_ssm), zeros_like_rows(state_hgrn))
    y_sample, s_shift, s_wkv, s_conv, s_ssm, s_hg = run(
        x_sample, state_rwkv_shift, state_rwkv_wkv, state_ssm_conv, state_ssm, state_hgrn)
    return (y_prompt, y_sample, p_shift, p_wkv, p_conv, p_ssm, p_hg, s_shift, s_wkv, s_conv, s_ssm, s_hg)
```

```python
import functools
import math

import jax
import jax.numpy as jnp
from jax import lax
from jax.experimental import pallas as pl
from jax.experimental.pallas import tpu as pltpu

F32 = jnp.float32
BF16 = jnp.bfloat16
HIGHEST = lax.Precision.HIGHEST

D_MODEL = 2048
DEPTH = 4
RW_HEAD = 64
RW_WIDTH = D_MODEL // 2
RW_HEADS = RW_WIDTH // RW_HEAD
RW_LORA_W = 64
RW_LORA_A = 64
RW_LORA_G = 160
RW_PROJ = 3 * RW_WIDTH + RW_LORA_W + RW_LORA_A + RW_LORA_G
RW_GN_EPS = 64e-5
SSM_WIDTH = D_MODEL
SSM_HEAD = 64
SSM_HEADS = SSM_WIDTH // SSM_HEAD
SSM_STATE = 128
SSM_GROUPS = 4
SSM_CONV = 4
SSM_CONV_CH = SSM_WIDTH + 2 * SSM_GROUPS * SSM_STATE
HG_WIDTH = D_MODEL
HG_EXPAND = 128
HG_HEADS = HG_WIDTH // HG_EXPAND
MOE_GROUPS = 4
MOE_PER_GROUP = 8
MOE_EXPERTS = MOE_GROUPS * MOE_PER_GROUP
MOE_TOPK = 2
MOE_HIDDEN = D_MODEL // 4
DN_ALPHA = (2 * DEPTH) ** 0.25
NORM_EPS = 1e-5

LANES = 128
VMEM_LIMIT = 52 * 1024 * 1024

WKV_CHUNK = 64
SSD_CHUNK = 128
GLA_CHUNK = 64
SAMPLE_PAD = 8
MM_TM = 1088
MM_TN = 512
MOE_ROWS = 256


def _bdot(a, b):
    return jnp.dot(a.astype(BF16), b.astype(BF16), preferred_element_type=F32)


def _bdot_nt(a, b):
    return lax.dot_general(a.astype(BF16), b.astype(BF16), (((1,), (1,)), ((), ())), preferred_element_type=F32)


def _bdot_tn(a, b):
    return lax.dot_general(a.astype(BF16), b.astype(BF16), (((0,), (0,)), ((), ())), preferred_element_type=F32)


def _split2(a):
    hi = a.astype(BF16)
    lo = (a - hi.astype(F32)).astype(BF16)
    return hi, lo


def _pdot(a, b):
    ah, al = _split2(a)
    bh, bl = _split2(b)
    d = lambda u, v: jnp.dot(u, v, preferred_element_type=F32)
    return d(ah, bh) + (d(ah, bl) + d(al, bh))


def _hdot(a, b):
    return jnp.dot(a, b, precision=HIGHEST, preferred_element_type=F32)


def _iota2(shape, axis):
    return lax.broadcasted_iota(jnp.int32, shape, axis)


def _mm_kernel(x_ref, w_ref, o_ref, *, exact):
    if exact:
        o_ref[...] = _hdot(x_ref[...], w_ref[...])
    else:
        o_ref[...] = _bdot(x_ref[...], w_ref[...])


def mm(x, w, *, tm=MM_TM, tn=MM_TN, exact=False):
    m, k = x.shape
    n = w.shape[1]
    tm = min(tm, m)
    tn = min(tn, n)
    return pl.pallas_call(
        functools.partial(_mm_kernel, exact=exact),
        grid=(pl.cdiv(m, tm), pl.cdiv(n, tn)),
        in_specs=[pl.BlockSpec((tm, k), lambda i, j: (i, 0)), pl.BlockSpec((k, tn), lambda i, j: (0, j))],
        out_specs=pl.BlockSpec((tm, tn), lambda i, j: (i, j)),
        out_shape=jax.ShapeDtypeStruct((m, n), F32),
        compiler_params=pltpu.CompilerParams(dimension_semantics=("parallel", "parallel"),
                                             vmem_limit_bytes=VMEM_LIMIT),
        name="mm",
    )(x, w)


def _wkv7_kernel(r_ref, lw_ref, k_ref, v_ref, kk_ref, a_ref, s0_ref, y_ref, so_ref, s_scr, *, chunk, heads):
    c = pl.program_id(2)

    @pl.when(c == 0)
    def _():
        s_scr[...] = s0_ref[0]

    row = _iota2((chunk, chunk), 0)
    col = _iota2((chunk, chunk), 1)
    incl = row >= col
    strict = row > col
    tri = incl.astype(F32)
    eye = (row == col).astype(F32)
    for h in range(heads):
        r, lw, k, v, kk, a = (ref[0, h] for ref in (r_ref, lw_ref, k_ref, v_ref, kk_ref, a_ref))
        s = s_scr[h]
        cum = _hdot(tri, lw)
        w_t = jnp.exp(cum)
        inv_w = jnp.exp(-cum)
        al = -kk * jnp.exp(cum - lw)
        be = kk * a * inv_w
        kh = k * inv_w
        rt = r * w_t
        amat = jnp.where(strict, _bdot_nt(al, be), 0.0)
        kmat = jnp.where(strict, _bdot_nt(al, kh), 0.0)
        tinv = eye + amat
        pw = amat
        for _ in range(int(math.log2(chunk)) - 1):
            pw = _pdot(pw, pw)
            tinv = tinv + _pdot(tinv, pw)
        z = _bdot_nt(al, s) + _bdot(kmat, v)
        u = _pdot(tinv, z)
        y = (_bdot_nt(rt, s) + _bdot(jnp.where(incl, _bdot_nt(rt, be), 0.0), u)
             + _bdot(jnp.where(incl, _bdot_nt(rt, kh), 0.0), v))
        y_ref[0, h] = y
        w_c = w_t[chunk - 1:chunk, :]
        s_scr[h] = s * w_c + _bdot_tn(u, be * w_c) + _bdot_tn(v, kh * w_c)

    @pl.when(c == pl.num_programs(2) - 1)
    def _():
        so_ref[0] = s_scr[...]


def wkv7(r, lw, k, v, kk, a, s0, *, chunk, heads):
    b, h, l, n = r.shape
    seq = pl.BlockSpec((1, heads, chunk, n), lambda i, j, c: (i, j, c, 0))
    st = pl.BlockSpec((1, heads, n, n), lambda i, j, c: (i, j, 0, 0))
    return pl.pallas_call(
        functools.partial(_wkv7_kernel, chunk=chunk, heads=heads),
        grid=(b, h // heads, l // chunk),
        in_specs=[seq] * 6 + [st],
        out_specs=[seq, st],
        out_shape=[jax.ShapeDtypeStruct((b, h, l, n), F32), jax.ShapeDtypeStruct((b, h, n, n), F32)],
        scratch_shapes=[pltpu.VMEM((heads, n, n), F32)],
        compiler_params=pltpu.CompilerParams(dimension_semantics=("parallel", "parallel", "arbitrary"),
                                             vmem_limit_bytes=VMEM_LIMIT),
        name="wkv7",
    )(r, lw, k, v, kk, a, s0)


def _ssd_kernel(x_ref, dt_ref, da_ref, dat_ref, b_ref, c_ref, s0_ref, y_ref, so_ref, s_scr, cumt_scr, *, chunk):
    hp = pl.program_id(1)
    c = pl.program_id(2)
    n_heads = dt_ref.shape[-1]
    pair = s_scr.shape[1]
    head = pair // 2

    @pl.when(c == 0)
    def _():
        s_scr[...] = s0_ref[0].reshape(pair, s_scr.shape[0]).T

    row = _iota2((chunk, chunk), 0)
    col = _iota2((chunk, chunk), 1)
    incl = row >= col
    cum_all = _hdot(incl.astype(F32), da_ref[0])
    cumt_scr[...] = _hdot(dat_ref[0], (row <= col).astype(F32))
    bm = b_ref[0]
    cm = c_ref[0]
    cb = _bdot_nt(cm, bm)
    x = x_ref[0]
    lane = _iota2((chunk, pair), 1)
    sel_row = _iota2((n_heads, pair), 0)
    s = s_scr[...]
    y = jnp.zeros((chunk, pair), F32)
    upd = jnp.zeros_like(s)
    cum_lanes = jnp.zeros((chunk, pair), F32)
    last_lanes = jnp.zeros((1, pair), F32)
    for j in range(2):
        hd = 2 * hp + j
        onehot = (sel_row == hd).astype(F32)
        cum_col = _hdot(cum_all, onehot)
        dt_col = _hdot(dt_ref[0], onehot)
        cum_row = cumt_scr[pl.ds(hd, 1), :]
        seg = cum_col[:, :chunk] - cum_row
        lmat = jnp.where(incl, jnp.exp(jnp.minimum(seg, 0.0)), 0.0)
        mine = (lane >= j * head) & (lane < (j + 1) * head)
        xdt = jnp.where(mine, x * dt_col, 0.0)
        y = y + _bdot(cb * lmat, xdt)
        last = cum_col[chunk - 1:chunk, :]
        to_end = jnp.exp(last - cum_col)
        upd = upd + _bdot_tn(bm * to_end, xdt)
        cum_lanes = jnp.where(mine, cum_col, cum_lanes)
        last_lanes = jnp.where(mine[:1], last, last_lanes)
    y = y + _bdot(cm, s) * jnp.exp(cum_lanes)
    y_ref[0] = y
    s_scr[...] = s * jnp.exp(last_lanes) + upd

    @pl.when(c == pl.num_programs(2) - 1)
    def _():
        so_ref[0] = s_scr[...].T.reshape(2, head, s_scr.shape[0])


def ssd(xs, dt, da, bm, cm, s0, *, chunk):
    b, l, width = xs.shape
    n_heads = dt.shape[-1]
    head = width // n_heads
    n_state = s0.shape[-1]
    groups = bm.shape[-1] // n_state
    pairs = n_heads // 2
    pairs_per_group = pairs // groups
    dat = jnp.swapaxes(da, 1, 2)
    small = pl.BlockSpec((1, chunk, n_heads), lambda i, j, c: (i, c, 0))
    grp = pl.BlockSpec((1, chunk, n_state), lambda i, j, c: (i, c, j // pairs_per_group))
    st = pl.BlockSpec((1, 2, head, n_state), lambda i, j, c: (i, j, 0, 0))
    xspec = pl.BlockSpec((1, chunk, 2 * head), lambda i, j, c: (i, c, j))
    return pl.pallas_call(
        functools.partial(_ssd_kernel, chunk=chunk),
        grid=(b, pairs, l // chunk),
        in_specs=[xspec, small, small, pl.BlockSpec((1, n_heads, chunk), lambda i, j, c: (i, 0, c)), grp, grp, st],
        out_specs=[xspec, st],
        out_shape=[jax.ShapeDtypeStruct(xs.shape, F32), jax.ShapeDtypeStruct(s0.shape, F32)],
        scratch_shapes=[pltpu.VMEM((n_state, 2 * head), F32), pltpu.VMEM((n_heads, chunk), F32)],
        compiler_params=pltpu.CompilerParams(dimension_semantics=("parallel", "parallel", "arbitrary"),
                                             vmem_limit_bytes=VMEM_LIMIT),
        name="ssd",
    )(xs, dt, da, dat, bm, cm, s0)


def _gla_levels(chunk):
    halves = []
    m = chunk // 2
    while m >= 1:
        halves.append(m)
        m //= 2
    return halves


def _gla_sum_matrix(chunk):
    t = jnp.arange(chunk)
    blocks = [t]
    for m in _gla_levels(chunk):
        blocks.append((t // (2 * m)) * (2 * m) + m - 1)
    upto = jnp.concatenate(blocks)
    return (jnp.arange(chunk)[None, :] <= upto[:, None]).astype(F32)


def _gla_kernel(q_ref, k_ref, v_ref, f_ref, g_ref, s0_ref, y_ref, so_ref, s_scr, *, chunk):
    c = pl.program_id(2)

    @pl.when(c == 0)
    def _():
        s_scr[...] = s0_ref[0, 0].T

    q, k, v = q_ref[0], k_ref[0], v_ref[0]
    sums = _hdot(g_ref[...], f_ref[0])
    cum = sums[:chunk]
    row = _iota2((chunk, chunk), 0)
    col = _iota2((chunk, chunk), 1)
    trow = _iota2((chunk, 1), 0)
    att = jnp.zeros((chunk, chunk), F32)
    for lvl, m in enumerate(_gla_levels(chunk)):
        ref = sums[(lvl + 1) * chunk:(lvl + 2) * chunk]
        second = (trow & (2 * m - 1)) >= m
        dq = jnp.where(second, jnp.minimum(cum - ref, 0.0), 0.0)
        dk = jnp.where(second, 0.0, jnp.minimum(ref - cum, 0.0))
        ql = jnp.where(second, q * jnp.exp(dq), 0.0)
        kl = jnp.where(second, 0.0, k * jnp.exp(dk))
        same = (row & -(2 * m)) == (col & -(2 * m))
        att = att + jnp.where(same, _bdot_nt(ql, kl), 0.0)
    diag = jnp.sum(q * k, axis=-1, keepdims=True)
    s = s_scr[...]
    y_ref[0] = _bdot(att, v) + diag * v + _bdot_nt(q * jnp.exp(cum), s)
    tot = cum[chunk - 1:chunk, :]
    s_scr[...] = s * jnp.exp(tot) + _bdot_tn(v, k * jnp.exp(tot - cum))

    @pl.when(c == pl.num_programs(2) - 1)
    def _():
        so_ref[0, 0] = s_scr[...].T


def gla(q, k, v, logf, s0, *, chunk):
    b, l, _ = q.shape
    h, kd, vd = s0.shape[1:]
    seq = pl.BlockSpec((1, chunk, kd), lambda i, j, c: (i, c, j))
    st = pl.BlockSpec((1, 1, kd, vd), lambda i, j, c: (i, j, 0, 0))
    gmat = _gla_sum_matrix(chunk)
    return pl.pallas_call(
        functools.partial(_gla_kernel, chunk=chunk),
        grid=(b, h, l // chunk),
        in_specs=[seq, seq, seq, seq, pl.BlockSpec(gmat.shape, lambda i, j, c: (0, 0)), st],
        out_specs=[seq, st],
        out_shape=[jax.ShapeDtypeStruct(v.shape, F32), jax.ShapeDtypeStruct(s0.shape, F32)],
        scratch_shapes=[pltpu.VMEM((vd, kd), F32)],
        compiler_params=pltpu.CompilerParams(dimension_semantics=("parallel", "parallel", "arbitrary"),
                                             vmem_limit_bytes=VMEM_LIMIT),
        name="gla",
    )(q, k, v, logf, gmat, s0)


def _moe_kernel(be_ref, used_ref, x_ref, g_ref, wg_ref, wu_ref, wd_ref, o_ref):
    i = pl.program_id(0)

    @pl.when(used_ref[i] > 0)
    def _():
        x = x_ref[...]
        hg = _bdot(x, wg_ref[0])
        hu = _bdot(x, wu_ref[0])
        hid = hg * jax.nn.sigmoid(hg) * hu
        o_ref[...] = _bdot(hid, wd_ref[0]) * g_ref[...]

    @pl.when(used_ref[i] == 0)
    def _():
        o_ref[...] = jnp.zeros_like(o_ref)


def moe_experts(block_e, block_used, x_rows, gate_rows, w_gate, w_up, w_down, *, rows):
    n_slots, d = x_rows.shape
    hid = w_gate.shape[-1]
    grid_spec = pltpu.PrefetchScalarGridSpec(
        num_scalar_prefetch=2,
        grid=(n_slots // rows,),
        in_specs=[
            pl.BlockSpec((rows, d), lambda i, be, used: (i, 0)),
            pl.BlockSpec((rows, 1), lambda i, be, used: (i, 0)),
            pl.BlockSpec((1, d, hid), lambda i, be, used: (be[i], 0, 0)),
            pl.BlockSpec((1, d, hid), lambda i, be, used: (be[i], 0, 0)),
            pl.BlockSpec((1, hid, d), lambda i, be, used: (be[i], 0, 0)),
        ],
        out_specs=pl.BlockSpec((rows, d), lambda i, be, used: (i, 0)),
    )
    return pl.pallas_call(
        _moe_kernel,
        grid_spec=grid_spec,
        out_shape=jax.ShapeDtypeStruct((n_slots, d), F32),
        compiler_params=pltpu.CompilerParams(dimension_semantics=("arbitrary",), vmem_limit_bytes=VMEM_LIMIT),
        name="moe_experts",
    )(block_e, block_used, x_rows, gate_rows, w_gate, w_up, w_down)


def hier_moe(x, wr_grp, br_grp, wr_exp, br_exp, w_gate, w_up, w_down):
    t, d = x.shape
    w_route = jnp.concatenate([wr_grp, jnp.moveaxis(wr_exp, 0, 1).reshape(d, MOE_EXPERTS)], axis=1)
    logits = mm(x, w_route, exact=True)
    grp_prob = jax.nn.softmax(logits[:, :MOE_GROUPS] + br_grp, axis=-1)
    grp_p, grp_i = lax.top_k(grp_prob, 1)
    exp_all = logits[:, MOE_GROUPS:].reshape(t, MOE_GROUPS, MOE_PER_GROUP) + br_exp
    exp_logit = jnp.take_along_axis(exp_all, grp_i[:, :, None], axis=1)[:, 0]
    top_p, top_i = lax.top_k(jax.nn.softmax(exp_logit, axis=-1), MOE_TOPK)
    gates = grp_p * top_p / jnp.sum(top_p, -1, keepdims=True)
    expert = grp_i * MOE_PER_GROUP + top_i
    m = t * MOE_TOPK
    flat_e = expert.reshape(m)
    onehot = (flat_e[:, None] == jnp.arange(MOE_EXPERTS)[None, :]).astype(jnp.int32)
    ranks = jnp.cumsum(onehot, axis=0) - onehot
    counts = jnp.sum(onehot, axis=0)
    padded = (counts + MOE_ROWS - 1) // MOE_ROWS * MOE_ROWS
    pad_ends = jnp.cumsum(padded)
    slot = jnp.sum(onehot * (ranks + (pad_ends - padded)[None, :]), axis=1)
    n_blocks = -(-m // MOE_ROWS) + MOE_EXPERTS
    n_slots = n_blocks * MOE_ROWS
    flat_t = jnp.repeat(jnp.arange(t, dtype=jnp.int32), MOE_TOPK)
    slot_tok = jnp.zeros((n_slots,), jnp.int32).at[slot].set(flat_t)
    slot_gate = jnp.zeros((n_slots,), F32).at[slot].set(gates.reshape(m))
    block_start = jnp.arange(n_blocks, dtype=jnp.int32) * MOE_ROWS
    block_e = jnp.minimum(jnp.searchsorted(pad_ends, block_start, side='right'), MOE_EXPERTS - 1).astype(jnp.int32)
    block_used = (block_start < pad_ends[-1]).astype(jnp.int32)
    x_rows = x.astype(BF16)[slot_tok]
    ys = moe_experts(block_e, block_used, x_rows, slot_gate[:, None], w_gate, w_up, w_down, rows=MOE_ROWS)
    return jnp.sum(ys[slot.reshape(t, MOE_TOPK)], axis=1)


def layer_norm(x, g, b):
    mu = jnp.mean(x, -1, keepdims=True)
    var = jnp.mean(jnp.square(x - mu), -1, keepdims=True)
    return (x - mu) * lax.rsqrt(var + NORM_EPS) * g + b


def group_rms_norm(x, w, n_groups):
    shp = x.shape
    xg = x.reshape(shp[:-1] + (n_groups, shp[-1] // n_groups))
    xg = xg * lax.rsqrt(jnp.mean(jnp.square(xg), -1, keepdims=True) + NORM_EPS)
    return xg.reshape(shp) * w


def _pad_len(t, length):
    return jnp.pad(t, [(0, 0), (0, length - t.shape[1])] + [(0, 0)] * (t.ndim - 2))


def _run_seq(fn, tensors, states, lens):
    (bp, lp, cp), (bs, ls, cs) = lens
    np_rows = bp * lp
    outs = []
    for (b, l, chunk), sl, st in (((bp, lp, cp), slice(0, np_rows), states[0]),
                                  ((bs, ls, cs), slice(np_rows, None), states[1])):
        seqs = [t[sl].reshape((b, l) + t.shape[1:]) for t in tensors]
        lpad = -(-l // chunk) * chunk
        y, s = fn([_pad_len(t, lpad) for t in seqs], st, chunk)
        outs.append((y[:, :l].reshape((b * l,) + y.shape[2:]), s))
    return jnp.concatenate([outs[0][0], outs[1][0]], axis=0), outs[0][1], outs[1][1]


def _shift_rows(p, prev, b, l):
    seq = p.reshape(b, l, p.shape[-1])
    return jnp.concatenate([prev[:, None, :], seq[:, :-1]], axis=1).reshape(p.shape), seq[:, -1]


def even_mixer(x, lens, shift0, wkv0, conv0, ssm0, v_first, vres, w_in, w_out, mu, w0, w_up, a0, a_up, g_up,
               k_k, k_a, r_k, lnx_g, lnx_b, conv_w, conv_b, dt_bias, a_log, d_skip, norm_w):
    (bp, lp, _), (bs, ls, _) = lens
    np_rows = bp * lp
    t = x.shape[0]
    p = mm(x.astype(BF16), w_in)
    p_rw, p_ssm = p[:, :RW_PROJ], p[:, RW_PROJ:]
    prev_p, shift_p = _shift_rows(p_rw[:np_rows], shift0[0], bp, lp)
    prev_s, shift_s = _shift_rows(p_rw[np_rows:], shift0[1], bs, ls)
    pm = p_rw + (jnp.concatenate([prev_p, prev_s], axis=0) - p_rw) * mu
    cuts = [RW_WIDTH, 2 * RW_WIDTH, 3 * RW_WIDTH, 3 * RW_WIDTH + RW_LORA_W, 3 * RW_WIDTH + RW_LORA_W + RW_LORA_A]
    r, k, v, wd, ad, gd = jnp.split(pm, cuts, axis=-1)
    log_w = -jax.nn.softplus(-(w0 + mm(jnp.tanh(wd), w_up))) - 0.5
    lw = -jnp.exp(log_w)
    if vres is None:
        v_first = v
    else:
        v0, v_down, v_up = vres
        v = v + (v_first - v) * jax.nn.sigmoid(v0 + mm(mm(v, v_down), v_up))
    a = jax.nn.sigmoid(a0 + mm(ad, a_up))
    g = mm(jax.nn.sigmoid(gd), g_up)
    heads = lambda u: u.reshape(t, RW_HEADS, RW_HEAD)
    kk = heads(k * k_k)
    kk = kk * lax.rsqrt(jnp.maximum(jnp.sum(jnp.square(kk), -1, keepdims=True), 1e-24))
    k = k * (1.0 + (a - 1.0) * k_a)
    rh, kh, vh = heads(r), heads(k), heads(v)

    def run_wkv(seqs, s0, chunk):
        y, s = wkv7(*[jnp.swapaxes(u, 1, 2) for u in seqs], s0, chunk=chunk, heads=4)
        return jnp.swapaxes(y, 1, 2), s

    o, wkv_p, wkv_s = _run_seq(run_wkv, [rh, heads(lw), kh, vh, kk, heads(a)], wkv0, lens_for(lens, WKV_CHUNK))
    mu_o = jnp.mean(o, -1, keepdims=True)
    var_o = jnp.mean(jnp.square(o - mu_o), -1, keepdims=True)
    o = (o - mu_o) * lax.rsqrt(var_o + RW_GN_EPS) * lnx_g.reshape(RW_HEADS, RW_HEAD) + lnx_b.reshape(RW_HEADS, RW_HEAD)
    o = o + jnp.sum(rh * kh * r_k, -1, keepdims=True) * vh
    o_rw = o.reshape(t, RW_WIDTH) * g
    z = p_ssm[:, :SSM_WIDTH]
    xbc_raw = p_ssm[:, SSM_WIDTH:SSM_WIDTH + SSM_CONV_CH]
    dt_raw = p_ssm[:, SSM_WIDTH + SSM_CONV_CH:]

    def conv(u, buf, b, l):
        full = jnp.concatenate([buf, u.reshape(b, l, SSM_CONV_CH)], axis=1)
        y = sum(full[:, j:j + l] * conv_w[j] for j in range(SSM_CONV)) + conv_b
        return y.reshape(b * l, SSM_CONV_CH), full[:, l:]

    xbc_p, conv_p = conv(xbc_raw[:np_rows], conv0[0], bp, lp)
    xbc_s, conv_s = conv(xbc_raw[np_rows:], conv0[1], bs, ls)
    xbc = jax.nn.silu(jnp.concatenate([xbc_p, xbc_s], axis=0))
    gn = SSM_GROUPS * SSM_STATE
    xs, bm, cm = xbc[:, :SSM_WIDTH], xbc[:, SSM_WIDTH:SSM_WIDTH + gn], xbc[:, SSM_WIDTH + gn:]
    dt = jax.nn.softplus(dt_raw + dt_bias)
    a_head = -jnp.exp(a_log)

    def run_ssd(seqs, s0, chunk):
        return ssd(*seqs, s0, chunk=chunk)

    y, ssm_p, ssm_s = _run_seq(run_ssd, [xs, dt, dt * a_head, bm, cm], ssm0, lens_for(lens, SSD_CHUNK))
    y = y + (xs.reshape(t, SSM_HEADS, SSM_HEAD) * d_skip[:, None]).reshape(t, SSM_WIDTH)
    y = group_rms_norm(y * jax.nn.silu(z), norm_w, SSM_GROUPS)
    mix = mm(jnp.concatenate([o_rw, y], axis=-1).astype(BF16), w_out, tm=MM_TM // 2)
    return mix, v_first, (shift_p, shift_s), (wkv_p, wkv_s), (conv_p, conv_s), (ssm_p, ssm_s)


def hgrn2_mixer(x, lens, s0, w_in, w_out, lower, norm_w):
    p = mm(x.astype(BF16), w_in)
    q, f, i, g = jnp.split(p, 4, axis=-1)
    fg = lower + (1.0 - lower) * jax.nn.sigmoid(f)

    def run_gla(seqs, st, chunk):
        return gla(*seqs, st, chunk=chunk)

    o, hg_p, hg_s = _run_seq(run_gla, [jax.nn.silu(q), 1.0 - fg, i, jnp.log(fg)], s0, lens_for(lens, GLA_CHUNK))
    o = group_rms_norm(o, norm_w, HG_HEADS) * jax.nn.silu(g)
    return mm(o.astype(BF16), w_out), (hg_p, hg_s)


def lens_for(lens, chunk):
    (bp, lp, _), (bs, ls, _) = lens
    return (bp, lp, min(chunk, lp)), (bs, ls, SAMPLE_PAD)


def kernel(x_prompt, x_sample, state_rwkv_shift, state_rwkv_wkv, state_ssm_conv, state_ssm, state_hgrn,
           ev_w_in, ev_w_out, rw_mu, rw_w0, rw_w_up, rw_a0, rw_a_up, rw_g_up, rw_k_k, rw_k_a, rw_r_k,
           rw_lnx_g, rw_lnx_b, rw_v0, rw_v_down, rw_v_up, ssm_conv_w, ssm_conv_b, ssm_dt_bias, ssm_a_log,
           ssm_d, ssm_norm_w, od_w_in, od_w_out, hg_lower_bounds, hg_norm_w, ln1_g, ln1_b, ln2_g, ln2_b,
           moe_wr_grp, moe_br_grp, moe_wr_exp, moe_br_exp, moe_w_gate, moe_w_up, moe_w_down):
    lb_soft = jax.nn.softmax(hg_lower_bounds, axis=0)
    lower = jnp.cumsum(lb_soft, axis=0) - lb_soft[0]
    bp, lp, d = x_prompt.shape
    bs, ls, _ = x_sample.shape
    lens = ((bp, lp, None), (bs, ls, None))
    x = jnp.concatenate([x_prompt.reshape(bp * lp, d), x_sample.reshape(bs * ls, d)], axis=0)
    zeros_p = lambda s: jnp.zeros((bp,) + s.shape[2:], s.dtype)
    v_first = None
    shifts, wkvs, convs, ssms, hgs = [], [], [], [], []
    for layer in range(DEPTH):
        if layer % 2 == 0:
            e = layer // 2
            vres = None if e == 0 else (rw_v0[e - 1], rw_v_down[e - 1], rw_v_up[e - 1])
            mix, v_first, s_sh, s_wkv, s_cv, s_ss = even_mixer(
                x, lens, (zeros_p(state_rwkv_shift), state_rwkv_shift[e]), (zeros_p(state_rwkv_wkv), state_rwkv_wkv[e]),
                (zeros_p(state_ssm_conv), state_ssm_conv[e]), (zeros_p(state_ssm), state_ssm[e]), v_first, vres,
                ev_w_in[e], ev_w_out[e], rw_mu[e], rw_w0[e], rw_w_up[e], rw_a0[e], rw_a_up[e], rw_g_up[e], rw_k_k[e],
                rw_k_a[e], rw_r_k[e], rw_lnx_g[e], rw_lnx_b[e], ssm_conv_w[e], ssm_conv_b[e], ssm_dt_bias[e],
                ssm_a_log[e], ssm_d[e], ssm_norm_w[e])
            shifts.append(s_sh)
            wkvs.append(s_wkv)
            convs.append(s_cv)
            ssms.append(s_ss)
        else:
            o = layer // 2
            mix, s_hg = hgrn2_mixer(x, lens, (zeros_p(state_hgrn), state_hgrn[o]), od_w_in[o], od_w_out[o],
                                    lower[layer], hg_norm_w[o])
            hgs.append(s_hg)
        x = layer_norm(DN_ALPHA * x + mix, ln1_g[layer], ln1_b[layer])
        ffn = hier_moe(x, moe_wr_grp[layer], moe_br_grp[layer], moe_wr_exp[layer], moe_br_exp[layer],
                       moe_w_gate[layer], moe_w_up[layer], moe_w_down[layer])
        x = layer_norm(DN_ALPHA * x + ffn, ln2_g[layer], ln2_b[layer])
    stack = lambda pairs, j: jnp.stack([p[j] for p in pairs])
    np_rows = bp * lp
    return (x[:np_rows].reshape(bp, lp, d), x[np_rows:].reshape(bs, ls, d),
            stack(shifts, 0), stack(wkvs, 0), stack(convs, 0), stack(ssms, 0), stack(hgs, 0),
            stack(shifts, 1), stack(wkvs, 1), stack(convs, 1), stack(ssms, 1), stack(hgs, 1))
```

```python
import functools
import math

import jax
import jax.numpy as jnp
from jax import lax
from jax.experimental import pallas as pl
from jax.experimental.pallas import tpu as pltpu

F32 = jnp.float32
BF16 = jnp.bfloat16
HIGHEST = lax.Precision.HIGHEST

D_MODEL = 2048
DEPTH = 4
RW_HEAD = 64
RW_WIDTH = D_MODEL // 2
RW_HEADS = RW_WIDTH // RW_HEAD
RW_LORA_W = 64
RW_LORA_A = 64
RW_LORA_G = 160
RW_PROJ = 3 * RW_WIDTH + RW_LORA_W + RW_LORA_A + RW_LORA_G
RW_GN_EPS = 64e-5
SSM_WIDTH = D_MODEL
SSM_HEAD = 64
SSM_HEADS = SSM_WIDTH // SSM_HEAD
SSM_STATE = 128
SSM_GROUPS = 4
SSM_CONV = 4
SSM_CONV_CH = SSM_WIDTH + 2 * SSM_GROUPS * SSM_STATE
HG_WIDTH = D_MODEL
HG_EXPAND = 128
HG_HEADS = HG_WIDTH // HG_EXPAND
MOE_GROUPS = 4
MOE_PER_GROUP = 8
MOE_EXPERTS = MOE_GROUPS * MOE_PER_GROUP
MOE_TOPK = 2
MOE_HIDDEN = D_MODEL // 4
DN_ALPHA = (2 * DEPTH) ** 0.25
NORM_EPS = 1e-5

LANES = 128
VMEM_LIMIT = 52 * 1024 * 1024

WKV_CHUNK = 64
SSD_CHUNK = 128
GLA_CHUNK = 64
GLA_HEADS_PER_STEP = 8
SAMPLE_PAD = 8
MM_TM = 1088
MM_TN = 512
MOE_ROWS = 256


def _bdot(a, b):
    return jnp.dot(a.astype(BF16), b.astype(BF16), preferred_element_type=F32)


def _bdot_nt(a, b):
    return lax.dot_general(a.astype(BF16), b.astype(BF16), (((1,), (1,)), ((), ())), preferred_element_type=F32)


def _bdot_tn(a, b):
    return lax.dot_general(a.astype(BF16), b.astype(BF16), (((0,), (0,)), ((), ())), preferred_element_type=F32)


def _split2(a):
    hi = a.astype(BF16)
    lo = (a - hi.astype(F32)).astype(BF16)
    return hi, lo


def _pdot(a, b):
    ah, al = _split2(a)
    bh, bl = _split2(b)
    d = lambda u, v: jnp.dot(u, v, preferred_element_type=F32)
    return d(ah, bh) + (d(ah, bl) + d(al, bh))


def _hdot(a, b):
    return jnp.dot(a, b, precision=HIGHEST, preferred_element_type=F32)


def _iota2(shape, axis):
    return lax.broadcasted_iota(jnp.int32, shape, axis)


def _mm_kernel(x_ref, w_ref, o_ref, *, exact):
    w = w_ref[...].reshape(w_ref.shape[-2:])
    if exact:
        o_ref[...] = _hdot(x_ref[...], w)
    else:
        o_ref[...] = _bdot(x_ref[...], w)


def mm(x, w, layer=None, *, tm=MM_TM, tn=MM_TN, exact=False):
    m, k = x.shape
    n = w.shape[-1]
    tm = min(tm, m)
    tn = min(tn, n)
    if layer is None:
        w_spec = pl.BlockSpec((k, tn), lambda i, j: (0, j))
    else:
        w_spec = pl.BlockSpec((1, k, tn), lambda i, j: (layer, 0, j))
    return pl.pallas_call(
        functools.partial(_mm_kernel, exact=exact),
        grid=(pl.cdiv(m, tm), pl.cdiv(n, tn)),
        in_specs=[pl.BlockSpec((tm, k), lambda i, j: (i, 0)), w_spec],
        out_specs=pl.BlockSpec((tm, tn), lambda i, j: (i, j)),
        out_shape=jax.ShapeDtypeStruct((m, n), F32),
        compiler_params=pltpu.CompilerParams(dimension_semantics=("parallel", "parallel"),
                                             vmem_limit_bytes=VMEM_LIMIT),
        name="mm",
    )(x, w)


def _wkv7_kernel(r_ref, lw_ref, k_ref, v_ref, kk_ref, a_ref, s0_ref, y_ref, so_ref, s_scr, *, chunk, heads):
    c = pl.program_id(2)

    @pl.when(c == 0)
    def _():
        s_scr[...] = s0_ref[0]

    row = _iota2((chunk, chunk), 0)
    col = _iota2((chunk, chunk), 1)
    incl = row >= col
    strict = row > col
    tri = incl.astype(F32)
    hs = range(heads)
    r, lw, k, v, kk, a = ([ref[0, h] for h in hs] for ref in (r_ref, lw_ref, k_ref, v_ref, kk_ref, a_ref))
    s = [s_scr[h] for h in hs]
    cum = [_hdot(tri, lw[h]) for h in hs]
    w_t = [jnp.exp(cum[h]) for h in hs]
    inv_w = [jnp.exp(-cum[h]) for h in hs]
    al = [-kk[h] * jnp.exp(cum[h] - lw[h]) for h in hs]
    be = [kk[h] * a[h] * inv_w[h] for h in hs]
    kh = [k[h] * inv_w[h] for h in hs]
    rt = [r[h] * w_t[h] for h in hs]
    pw = [jnp.where(strict, _bdot_nt(al[h], be[h]), 0.0) for h in hs]
    kmat = [jnp.where(strict, _bdot_nt(al[h], kh[h]), 0.0) for h in hs]
    bmat = [jnp.where(incl, _bdot_nt(rt[h], be[h]), 0.0) for h in hs]
    vmat = [jnp.where(incl, _bdot_nt(rt[h], kh[h]), 0.0) for h in hs]
    z = [_bdot_nt(al[h], s[h]) + _bdot(kmat[h], v[h]) for h in hs]
    y0 = [_bdot_nt(rt[h], s[h]) + _bdot(vmat[h], v[h]) for h in hs]
    u = [z[h] + _pdot(pw[h], z[h]) for h in hs]
    for _ in range(int(math.log2(chunk)) - 1):
        pw = [_pdot(pw[h], pw[h]) for h in hs]
        u = [u[h] + _pdot(pw[h], u[h]) for h in hs]
    y = [y0[h] + _bdot(bmat[h], u[h]) for h in hs]
    w_c = [w_t[h][chunk - 1:chunk, :] for h in hs]
    s_new = [s[h] * w_c[h] + _bdot_tn(u[h], be[h] * w_c[h]) + _bdot_tn(v[h], kh[h] * w_c[h]) for h in hs]
    for h in hs:
        y_ref[0, h] = y[h]
        s_scr[h] = s_new[h]

    @pl.when(c == pl.num_programs(2) - 1)
    def _():
        so_ref[0] = s_scr[...]


def _state_spec(block, index, layer):
    if layer is None:
        return pl.BlockSpec(block, index)
    return pl.BlockSpec((None,) + block, lambda *g: (layer,) + index(*g))


def wkv7(r, lw, k, v, kk, a, s0, layer=None, *, chunk, heads):
    b, h, l, n = r.shape
    seq = pl.BlockSpec((1, heads, chunk, n), lambda i, j, c: (i, j, c, 0))
    st = pl.BlockSpec((1, heads, n, n), lambda i, j, c: (i, j, 0, 0))
    return pl.pallas_call(
        functools.partial(_wkv7_kernel, chunk=chunk, heads=heads),
        grid=(b, h // heads, l // chunk),
        in_specs=[seq] * 6 + [_state_spec((1, heads, n, n), lambda i, j, c: (i, j, 0, 0), layer)],
        out_specs=[seq, st],
        out_shape=[jax.ShapeDtypeStruct((b, h, l, n), F32), jax.ShapeDtypeStruct((b, h, n, n), F32)],
        scratch_shapes=[pltpu.VMEM((heads, n, n), F32)],
        compiler_params=pltpu.CompilerParams(dimension_semantics=("parallel", "parallel", "arbitrary"),
                                             vmem_limit_bytes=VMEM_LIMIT),
        name="wkv7",
    )(r, lw, k, v, kk, a, s0)


def _ssd_kernel(x_ref, dt_ref, da_ref, dat_ref, b_ref, c_ref, e_ref, s0_ref, y_ref, so_ref, s_scr, *, chunk):
    c = pl.program_id(1)
    pairs, pair, n_state = s_scr.shape
    head = pair // 2
    pairs_per_group = pairs * n_state // b_ref.shape[-1]

    @pl.when(c == 0)
    def _():
        s_scr[...] = s0_ref[0].reshape(pairs, pair, n_state)

    row = _iota2((chunk, chunk), 0)
    col = _iota2((chunk, chunk), 1)
    incl = row >= col
    cum_all = _hdot(incl.astype(F32), da_ref[0])
    cum_t = _hdot(dat_ref[0], (row <= col).astype(F32))
    expand = e_ref[...]
    cum_lanes = _hdot(cum_all, expand)
    dt_lanes = _hdot(dt_ref[0], expand)
    xdt = x_ref[0] * dt_lanes
    xdt_end = xdt * jnp.exp(cum_lanes[chunk - 1:chunk] - cum_lanes)
    ecum = jnp.exp(cum_lanes)
    bm = b_ref[0]
    cm = c_ref[0]
    group = lambda t, p: t[:, (p // pairs_per_group) * n_state:(p // pairs_per_group + 1) * n_state]
    cb = [_bdot_nt(cm[:, g * n_state:(g + 1) * n_state], bm[:, g * n_state:(g + 1) * n_state])
          for g in range(pairs // pairs_per_group)]

    def decay_scores(h):
        seg = cum_all[:, h:h + 1] - cum_t[h:h + 1, :]
        return cb[h // (2 * pairs_per_group)] * jnp.where(incl, jnp.exp(jnp.minimum(seg, 0.0)), 0.0)

    ps = range(pairs)
    left = _iota2((chunk, pair), 1) < head
    top = _iota2((pair, n_state), 0) < head
    s = [s_scr[p] for p in ps]
    m0 = [decay_scores(2 * p) for p in ps]
    m1 = [decay_scores(2 * p + 1) for p in ps]
    xp = [xdt[:, p * pair:(p + 1) * pair] for p in ps]
    if chunk % LANES == 0:
        intra = [_bdot(jnp.concatenate([m0[p], m1[p]], axis=1),
                       jnp.concatenate([jnp.where(left, xp[p], 0.0), jnp.where(left, 0.0, xp[p])], axis=0)) for p in ps]
    else:
        intra = [_bdot(m0[p], jnp.where(left, xp[p], 0.0)) + _bdot(m1[p], jnp.where(left, 0.0, xp[p])) for p in ps]
    inter = [_bdot_nt(group(cm, p), s[p]) * ecum[:, p * pair:(p + 1) * pair] for p in ps]
    last = jnp.exp(cum_all[chunk - 1:chunk, :])
    dec = [jnp.where(top, last[:, 2 * p:2 * p + 1], last[:, 2 * p + 1:2 * p + 2]) for p in ps]
    s_new = [s[p] * dec[p] + _bdot_tn(xdt_end[:, p * pair:(p + 1) * pair], group(bm, p)) for p in ps]
    for p in ps:
        y_ref[0, :, p * pair:(p + 1) * pair] = intra[p] + inter[p]
        s_scr[p] = s_new[p]

    @pl.when(c == pl.num_programs(1) - 1)
    def _():
        so_ref[0] = s_scr[...].reshape(2 * pairs, head, n_state)


def ssd(xs, dt, da, bm, cm, s0, layer=None, *, chunk):
    b, l, width = xs.shape
    n_heads = dt.shape[-1]
    head = width // n_heads
    n_state = s0.shape[-1]
    dat = jnp.swapaxes(da, 1, 2)
    expand = (jnp.arange(width)[None, :] // head == jnp.arange(n_heads)[:, None]).astype(F32)
    seq = lambda w: pl.BlockSpec((1, chunk, w), lambda i, c: (i, c, 0))
    st = pl.BlockSpec((1, n_heads, head, n_state), lambda i, c: (i, 0, 0, 0))
    return pl.pallas_call(
        functools.partial(_ssd_kernel, chunk=chunk),
        grid=(b, l // chunk),
        in_specs=[seq(width), seq(n_heads), seq(n_heads), pl.BlockSpec((1, n_heads, chunk), lambda i, c: (i, 0, c)),
                  seq(bm.shape[-1]), seq(cm.shape[-1]), pl.BlockSpec(expand.shape, lambda i, c: (0, 0)),
                  _state_spec((1, n_heads, head, n_state), lambda i, c: (i, 0, 0, 0), layer)],
        out_specs=[seq(width), st],
        out_shape=[jax.ShapeDtypeStruct(xs.shape, F32), jax.ShapeDtypeStruct(s0.shape[-4:], F32)],
        scratch_shapes=[pltpu.VMEM((n_heads // 2, 2 * head, n_state), F32)],
        compiler_params=pltpu.CompilerParams(dimension_semantics=("parallel", "arbitrary"),
                                             vmem_limit_bytes=VMEM_LIMIT),
        name="ssd",
    )(xs, dt, da, dat, bm, cm, expand, s0)


def _gla_levels(chunk):
    halves = []
    m = chunk // 2
    while m >= 1:
        halves.append(m)
        m //= 2
    return halves


def _gla_sum_matrix(chunk):
    t = jnp.arange(chunk)
    blocks = [t]
    for m in _gla_levels(chunk):
        blocks.append((t // (2 * m)) * (2 * m) + m - 1)
    upto = jnp.concatenate(blocks)
    return (jnp.arange(chunk)[None, :] <= upto[:, None]).astype(F32)


def _gla_kernel(q_ref, k_ref, v_ref, f_ref, g_ref, s0_ref, y_ref, so_ref, s_scr, *, chunk, heads):
    c = pl.program_id(2)
    kd = s_scr.shape[1]

    @pl.when(c == 0)
    def _():
        s_scr[...] = s0_ref[0]

    row = _iota2((chunk, chunk), 0)
    col = _iota2((chunk, chunk), 1)
    trow = _iota2((chunk, 1), 0)
    hs = range(heads)
    lanes = lambda ref, h: ref[0, :, h * kd:(h + 1) * kd]
    q, k, v = ([lanes(ref, h) for h in hs] for ref in (q_ref, k_ref, v_ref))
    s = [s_scr[h] for h in hs]
    gmat = g_ref[...]
    sums = [_hdot(gmat, lanes(f_ref, h)) for h in hs]
    cum = [sums[h][:chunk] for h in hs]
    att = [jnp.zeros((chunk, chunk), F32) for _ in hs]
    for lvl, m in enumerate(_gla_levels(chunk)):
        second = (trow & (2 * m - 1)) >= m
        same = (row & -(2 * m)) == (col & -(2 * m))
        ref = [sums[h][(lvl + 1) * chunk:(lvl + 2) * chunk] for h in hs]
        ql = [jnp.where(second, q[h] * jnp.exp(jnp.where(second, jnp.minimum(cum[h] - ref[h], 0.0), 0.0)), 0.0)
              for h in hs]
        kl = [jnp.where(second, 0.0, k[h] * jnp.exp(jnp.where(second, 0.0, jnp.minimum(ref[h] - cum[h], 0.0))))
              for h in hs]
        att = [att[h] + jnp.where(same, _bdot_nt(ql[h], kl[h]), 0.0) for h in hs]
    diag = [jnp.sum(q[h] * k[h], axis=-1, keepdims=True) for h in hs]
    y = [_bdot(att[h], v[h]) + diag[h] * v[h] + _bdot(q[h] * jnp.exp(cum[h]), s[h]) for h in hs]
    tot = [cum[h][chunk - 1:chunk, :] for h in hs]
    dec = [jnp.broadcast_to(jnp.exp(tot[h]), (8, kd)).T[:, :1] for h in hs]
    s_new = [s[h] * dec[h] + _bdot_tn(k[h] * jnp.exp(tot[h] - cum[h]), v[h]) for h in hs]
    for h in hs:
        y_ref[0, :, h * kd:(h + 1) * kd] = y[h]
        s_scr[h] = s_new[h]

    @pl.when(c == pl.num_programs(2) - 1)
    def _():
        so_ref[0] = s_scr[...]


def gla(q, k, v, logf, s0, layer=None, *, chunk, heads):
    b, l, _ = q.shape
    h, kd, vd = s0.shape[-3:]
    seq = pl.BlockSpec((1, chunk, heads * kd), lambda i, j, c: (i, c, j))
    st = pl.BlockSpec((1, heads, kd, vd), lambda i, j, c: (i, j, 0, 0))
    gmat = _gla_sum_matrix(chunk)
    return pl.pallas_call(
        functools.partial(_gla_kernel, chunk=chunk, heads=heads),
        grid=(b, h // heads, l // chunk),
        in_specs=[seq, seq, seq, seq, pl.BlockSpec(gmat.shape, lambda i, j, c: (0, 0)),
                  _state_spec((1, heads, kd, vd), lambda i, j, c: (i, j, 0, 0), layer)],
        out_specs=[seq, st],
        out_shape=[jax.ShapeDtypeStruct(v.shape, F32), jax.ShapeDtypeStruct(s0.shape[-4:], F32)],
        scratch_shapes=[pltpu.VMEM((heads, kd, vd), F32)],
        compiler_params=pltpu.CompilerParams(dimension_semantics=("parallel", "parallel", "arbitrary"),
                                             vmem_limit_bytes=VMEM_LIMIT),
        name="gla",
    )(q, k, v, logf, gmat, s0)


def _moe_kernel(be_ref, used_ref, x_ref, wg_ref, wu_ref, wd_ref, o_ref):
    i = pl.program_id(0)

    @pl.when(used_ref[i] > 0)
    def _():
        x = x_ref[...]
        hg = _bdot(x, wg_ref[0, 0])
        hu = _bdot(x, wu_ref[0, 0])
        hid = hg * jax.nn.sigmoid(hg) * hu
        o_ref[...] = _bdot(hid, wd_ref[0, 0])

    @pl.when(used_ref[i] == 0)
    def _():
        o_ref[...] = jnp.zeros_like(o_ref)


def moe_experts(block_e, block_used, x_rows, w_gate, w_up, w_down, layer, *, rows):
    n_slots, d = x_rows.shape
    hid = w_gate.shape[-1]
    grid_spec = pltpu.PrefetchScalarGridSpec(
        num_scalar_prefetch=2,
        grid=(n_slots // rows,),
        in_specs=[
            pl.BlockSpec((rows, d), lambda i, be, used: (i, 0)),
            pl.BlockSpec((1, 1, d, hid), lambda i, be, used: (layer, be[i], 0, 0)),
            pl.BlockSpec((1, 1, d, hid), lambda i, be, used: (layer, be[i], 0, 0)),
            pl.BlockSpec((1, 1, hid, d), lambda i, be, used: (layer, be[i], 0, 0)),
        ],
        out_specs=pl.BlockSpec((rows, d), lambda i, be, used: (i, 0)),
    )
    return pl.pallas_call(
        _moe_kernel,
        grid_spec=grid_spec,
        out_shape=jax.ShapeDtypeStruct((n_slots, d), F32),
        compiler_params=pltpu.CompilerParams(dimension_semantics=("arbitrary",), vmem_limit_bytes=VMEM_LIMIT),
        name="moe_experts",
    )(block_e, block_used, x_rows, w_gate, w_up, w_down)


def _first_max(p):
    idx = lax.broadcasted_iota(jnp.int32, p.shape, p.ndim - 1)
    top = jnp.max(p, axis=-1, keepdims=True)
    return top, jnp.min(jnp.where(p == top, idx, p.shape[-1]), axis=-1, keepdims=True)


def _prefix_counts(onehot):
    m, e = onehot.shape
    blk = LANES
    oh = onehot.reshape(m // blk, blk, e)
    r = jnp.arange(blk)
    inside = jnp.einsum('ts,bse->bte', (r[:, None] > r[None, :]).astype(F32), oh, precision=HIGHEST)
    totals = jnp.sum(oh, axis=1)
    b = jnp.arange(m // blk)
    before = jnp.einsum('ab,be->ae', (b[:, None] > b[None, :]).astype(F32), totals, precision=HIGHEST)
    return (inside + before[:, None, :]).reshape(m, e), jnp.sum(totals, axis=0)


def hier_moe(x, wr_grp, br_grp, wr_exp, br_exp, w_gate, w_up, w_down, layer):
    t, d = x.shape
    w_route = jnp.concatenate([wr_grp, jnp.moveaxis(wr_exp, 0, 1).reshape(d, MOE_EXPERTS)], axis=1)
    logits = mm(x, w_route, exact=True)
    grp_p, grp_i = _first_max(jax.nn.softmax(logits[:, :MOE_GROUPS] + br_grp, axis=-1))
    exp_all = logits[:, MOE_GROUPS:].reshape(t, MOE_GROUPS, MOE_PER_GROUP) + br_exp
    in_grp = lax.broadcasted_iota(jnp.int32, (t, MOE_GROUPS, 1), 1) == grp_i[:, :, None]
    exp_prob = jax.nn.softmax(jnp.sum(jnp.where(in_grp, exp_all, 0.0), axis=1), axis=-1)
    p1, i1 = _first_max(exp_prob)
    taken = lax.broadcasted_iota(jnp.int32, exp_prob.shape, 1) == i1
    p2, i2 = _first_max(jnp.where(taken, -jnp.inf, exp_prob))
    top_p = jnp.concatenate([p1, p2], axis=1)
    gates = grp_p * top_p / jnp.sum(top_p, -1, keepdims=True)
    expert = grp_i * MOE_PER_GROUP + jnp.concatenate([i1, i2], axis=1)
    m = t * MOE_TOPK
    onehot = (expert.reshape(m, 1) == jnp.arange(MOE_EXPERTS)[None, :]).astype(F32)
    ranks, counts = _prefix_counts(onehot)
    padded = jnp.ceil(counts / MOE_ROWS) * MOE_ROWS
    ends = jnp.sum(jnp.where(jnp.arange(MOE_EXPERTS)[:, None] <= jnp.arange(MOE_EXPERTS)[None, :],
                             padded[:, None], 0.0), axis=0)
    slot = jnp.sum(onehot * (ranks + (ends - padded)[None, :]), axis=1).astype(jnp.int32)
    n_blocks = -(-m // MOE_ROWS) + MOE_EXPERTS
    n_slots = n_blocks * MOE_ROWS
    flat_t = lax.broadcasted_iota(jnp.int32, (t, MOE_TOPK), 0).reshape(m)
    slot_tok = jnp.zeros((n_slots,), jnp.int32).at[slot].set(flat_t)
    block_start = jnp.arange(n_blocks, dtype=jnp.int32) * MOE_ROWS
    ends_i = ends.astype(jnp.int32)
    block_e = jnp.minimum(jnp.sum((block_start[:, None] >= ends_i[None, :]).astype(jnp.int32), axis=1),
                          MOE_EXPERTS - 1)
    block_used = (block_start < ends_i[-1]).astype(jnp.int32)
    ys = moe_experts(block_e, block_used, x[slot_tok], w_gate, w_up, w_down, layer, rows=MOE_ROWS)
    slot = slot.reshape(t, MOE_TOPK)
    return ys[slot[:, 0]] * gates[:, :1] + ys[slot[:, 1]] * gates[:, 1:]


def layer_norm(x, g, b):
    mu = jnp.mean(x, -1, keepdims=True)
    var = jnp.mean(jnp.square(x - mu), -1, keepdims=True)
    return (x - mu) * lax.rsqrt(var + NORM_EPS) * g + b


def group_rms_norm(x, w, n_groups):
    shp = x.shape
    xg = x.reshape(shp[:-1] + (n_groups, shp[-1] // n_groups))
    xg = xg * lax.rsqrt(jnp.mean(jnp.square(xg), -1, keepdims=True) + NORM_EPS)
    return xg.reshape(shp) * w


def _pad_len(t, length):
    return jnp.pad(t, [(0, 0), (0, length - t.shape[1])] + [(0, 0)] * (t.ndim - 2))


def _run_seq(fn, tensors, states, lens):
    (bp, lp, cp), (bs, ls, cs) = lens
    np_rows = bp * lp
    outs = []
    for (b, l, chunk), sl, st in (((bp, lp, cp), slice(0, np_rows), states[0]),
                                  ((bs, ls, cs), slice(np_rows, None), states[1])):
        seqs = [t[sl].reshape((b, l) + t.shape[1:]) for t in tensors]
        lpad = -(-l // chunk) * chunk
        y, s = fn([_pad_len(t, lpad) for t in seqs], st, chunk)
        outs.append((y[:, :l].reshape((b * l,) + y.shape[2:]), s))
    return jnp.concatenate([outs[0][0], outs[1][0]], axis=0), outs[0][1], outs[1][1]


def _shift_rows(p, prev, b, l):
    seq = p.reshape(b, l, p.shape[-1])
    return jnp.concatenate([prev[:, None, :], seq[:, :-1]], axis=1).reshape(p.shape), seq[:, -1]


def even_mixer(x, lens, e, shift0, wkv0, conv0, ssm0, v_first, vres, w_in, w_out, mu, w0, w_up, a0, a_up, g_up,
               k_k, k_a, r_k, lnx_g, lnx_b, conv_w, conv_b, dt_bias, a_log, d_skip, norm_w):
    (bp, lp, _), (bs, ls, _) = lens
    np_rows = bp * lp
    t = x.shape[0]
    p = mm(x.astype(BF16), w_in, e)
    p_rw, p_ssm = p[:, :RW_PROJ], p[:, RW_PROJ:]
    prev_p, shift_p = _shift_rows(p_rw[:np_rows], shift0[0], bp, lp)
    prev_s, shift_s = _shift_rows(p_rw[np_rows:], shift0[1], bs, ls)
    pm = p_rw + (jnp.concatenate([prev_p, prev_s], axis=0) - p_rw) * mu
    cuts = [RW_WIDTH, 2 * RW_WIDTH, 3 * RW_WIDTH, 3 * RW_WIDTH + RW_LORA_W, 3 * RW_WIDTH + RW_LORA_W + RW_LORA_A]
    r, k, v, wd, ad, gd = jnp.split(pm, cuts, axis=-1)
    log_w = -jax.nn.softplus(-(w0 + mm(jnp.tanh(wd), w_up))) - 0.5
    lw = -jnp.exp(log_w)
    if vres is None:
        v_first = v
    else:
        v0, v_down, v_up = vres
        v = v + (v_first - v) * jax.nn.sigmoid(v0 + mm(mm(v, v_down), v_up))
    a = jax.nn.sigmoid(a0 + mm(ad, a_up))
    g = mm(jax.nn.sigmoid(gd), g_up)
    heads = lambda u: u.reshape(t, RW_HEADS, RW_HEAD)
    kk = heads(k * k_k)
    kk = kk * lax.rsqrt(jnp.maximum(jnp.sum(jnp.square(kk), -1, keepdims=True), 1e-24))
    k = k * (1.0 + (a - 1.0) * k_a)
    rh, kh, vh = heads(r), heads(k), heads(v)

    def run_wkv(seqs, s0, chunk):
        y, s = wkv7(*[jnp.swapaxes(u, 1, 2) for u in seqs], *s0, chunk=chunk, heads=RW_HEADS)
        return jnp.swapaxes(y, 1, 2), s

    o, wkv_p, wkv_s = _run_seq(run_wkv, [rh, heads(lw), kh, vh, kk, heads(a)], wkv0, lens_for(lens, WKV_CHUNK))
    mu_o = jnp.mean(o, -1, keepdims=True)
    var_o = jnp.mean(jnp.square(o - mu_o), -1, keepdims=True)
    o = (o - mu_o) * lax.rsqrt(var_o + RW_GN_EPS) * lnx_g.reshape(RW_HEADS, RW_HEAD) + lnx_b.reshape(RW_HEADS, RW_HEAD)
    o = o + jnp.sum(rh * kh * r_k, -1, keepdims=True) * vh
    o_rw = o.reshape(t, RW_WIDTH) * g
    z = p_ssm[:, :SSM_WIDTH]
    xbc_raw = p_ssm[:, SSM_WIDTH:SSM_WIDTH + SSM_CONV_CH]
    dt_raw = p_ssm[:, SSM_WIDTH + SSM_CONV_CH:]

    def conv(u, buf, b, l):
        full = jnp.concatenate([buf, u.reshape(b, l, SSM_CONV_CH)], axis=1)
        y = sum(full[:, j:j + l] * conv_w[j] for j in range(SSM_CONV)) + conv_b
        return y.reshape(b * l, SSM_CONV_CH), full[:, l:]

    xbc_p, conv_p = conv(xbc_raw[:np_rows], conv0[0], bp, lp)
    xbc_s, conv_s = conv(xbc_raw[np_rows:], conv0[1], bs, ls)
    xbc = jax.nn.silu(jnp.concatenate([xbc_p, xbc_s], axis=0))
    gn = SSM_GROUPS * SSM_STATE
    xs, bm, cm = xbc[:, :SSM_WIDTH], xbc[:, SSM_WIDTH:SSM_WIDTH + gn], xbc[:, SSM_WIDTH + gn:]
    dt = jax.nn.softplus(dt_raw + dt_bias)
    a_head = -jnp.exp(a_log)

    def run_ssd(seqs, s0, chunk):
        return ssd(*seqs, *s0, chunk=chunk)

    y, ssm_p, ssm_s = _run_seq(run_ssd, [xs, dt, dt * a_head, bm, cm], ssm0, lens_for(lens, SSD_CHUNK))
    y = y + (xs.reshape(t, SSM_HEADS, SSM_HEAD) * d_skip[:, None]).reshape(t, SSM_WIDTH)
    y = group_rms_norm(y * jax.nn.silu(z), norm_w, SSM_GROUPS)
    mix = mm(jnp.concatenate([o_rw, y], axis=-1).astype(BF16), w_out, e, tm=MM_TM // 2)
    return mix, v_first, (shift_p, shift_s), (wkv_p, wkv_s), (conv_p, conv_s), (ssm_p, ssm_s)


def hgrn2_mixer(x, lens, o, s0, w_in, w_out, lower, norm_w):
    p = mm(x.astype(BF16), w_in, o)
    q, f, i, g = jnp.split(p, 4, axis=-1)
    fg = lower + (1.0 - lower) * jax.nn.sigmoid(f)

    def run_gla(seqs, st, chunk):
        return gla(*seqs, *st, chunk=chunk, heads=GLA_HEADS_PER_STEP if chunk == GLA_CHUNK else HG_HEADS)

    y, hg_p, hg_s = _run_seq(run_gla, [jax.nn.silu(q), 1.0 - fg, i, jnp.log(fg)], s0, lens_for(lens, GLA_CHUNK))
    y = group_rms_norm(y, norm_w, HG_HEADS) * jax.nn.silu(g)
    return mm(y.astype(BF16), w_out, o), (hg_p, hg_s)


def lens_for(lens, chunk):
    (bp, lp, _), (bs, ls, _) = lens
    return (bp, lp, min(chunk, lp)), (bs, ls, SAMPLE_PAD)


def kernel(x_prompt, x_sample, state_rwkv_shift, state_rwkv_wkv, state_ssm_conv, state_ssm, state_hgrn,
           ev_w_in, ev_w_out, rw_mu, rw_w0, rw_w_up, rw_a0, rw_a_up, rw_g_up, rw_k_k, rw_k_a, rw_r_k,
           rw_lnx_g, rw_lnx_b, rw_v0, rw_v_down, rw_v_up, ssm_conv_w, ssm_conv_b, ssm_dt_bias, ssm_a_log,
           ssm_d, ssm_norm_w, od_w_in, od_w_out, hg_lower_bounds, hg_norm_w, ln1_g, ln1_b, ln2_g, ln2_b,
           moe_wr_grp, moe_br_grp, moe_wr_exp, moe_br_exp, moe_w_gate, moe_w_up, moe_w_down):
    lb_soft = jax.nn.softmax(hg_lower_bounds, axis=0)
    lower = jnp.cumsum(lb_soft, axis=0) - lb_soft[0]
    bp, lp, d = x_prompt.shape
    bs, ls, _ = x_sample.shape
    lens = ((bp, lp, None), (bs, ls, None))
    x = jnp.concatenate([x_prompt.reshape(bp * lp, d), x_sample.reshape(bs * ls, d)], axis=0)
    zeros_p = lambda s: jnp.zeros((bp,) + s.shape[2:], s.dtype)
    v_first = None
    shifts, wkvs, convs, ssms, hgs = [], [], [], [], []
    for layer in range(DEPTH):
        if layer % 2 == 0:
            e = layer // 2
            vres = None if e == 0 else (rw_v0[e - 1], rw_v_down[e - 1], rw_v_up[e - 1])
            mix, v_first, s_sh, s_wkv, s_cv, s_ss = even_mixer(
                x, lens, e, (zeros_p(state_rwkv_shift), state_rwkv_shift[e]),
                ((zeros_p(state_rwkv_wkv), None), (state_rwkv_wkv, e)),
                (zeros_p(state_ssm_conv), state_ssm_conv[e]), ((zeros_p(state_ssm), None), (state_ssm, e)), v_first, vres,
                ev_w_in, ev_w_out, rw_mu[e], rw_w0[e], rw_w_up[e], rw_a0[e], rw_a_up[e], rw_g_up[e], rw_k_k[e],
                rw_k_a[e], rw_r_k[e], rw_lnx_g[e], rw_lnx_b[e], ssm_conv_w[e], ssm_conv_b[e], ssm_dt_bias[e],
                ssm_a_log[e], ssm_d[e], ssm_norm_w[e])
            shifts.append(s_sh)
            wkvs.append(s_wkv)
            convs.append(s_cv)
            ssms.append(s_ss)
        else:
            o = layer // 2
            mix, s_hg = hgrn2_mixer(x, lens, o, ((zeros_p(state_hgrn), None), (state_hgrn, o)), od_w_in, od_w_out,
                                    lower[layer], hg_norm_w[o])
            hgs.append(s_hg)
        x = layer_norm(DN_ALPHA * x + mix, ln1_g[layer], ln1_b[layer])
        ffn = hier_moe(x, moe_wr_grp[layer], moe_br_grp[layer], moe_wr_exp[layer], moe_br_exp[layer],
                       moe_w_gate, moe_w_up, moe_w_down, layer)
        x = layer_norm(DN_ALPHA * x + ffn, ln2_g[layer], ln2_b[layer])
    stack = lambda pairs, j: jnp.stack([p[j] for p in pairs])
    np_rows = bp * lp
    return (x[:np_rows].reshape(bp, lp, d), x[np_rows:].reshape(bs, ls, d),
            stack(shifts, 0), stack(wkvs, 0), stack(convs, 0), stack(ssms, 0), stack(hgs, 0),
            stack(shifts, 1), stack(wkvs, 1), stack(convs, 1), stack(ssms, 1), stack(hgs, 1))
```

```python
import functools
import math

import jax
import jax.numpy as jnp
from jax import lax
from jax.experimental import pallas as pl
from jax.experimental.pallas import tpu as pltpu

F32 = jnp.float32
BF16 = jnp.bfloat16
HIGHEST = lax.Precision.HIGHEST

D_MODEL = 2048
DEPTH = 4
RW_HEAD = 64
RW_WIDTH = D_MODEL // 2
RW_HEADS = RW_WIDTH // RW_HEAD
RW_LORA_W = 64
RW_LORA_A = 64
RW_LORA_G = 160
RW_PROJ = 3 * RW_WIDTH + RW_LORA_W + RW_LORA_A + RW_LORA_G
RW_GN_EPS = 64e-5
SSM_WIDTH = D_MODEL
SSM_HEAD = 64
SSM_HEADS = SSM_WIDTH // SSM_HEAD
SSM_STATE = 128
SSM_GROUPS = 4
SSM_CONV = 4
SSM_CONV_CH = SSM_WIDTH + 2 * SSM_GROUPS * SSM_STATE
HG_WIDTH = D_MODEL
HG_EXPAND = 128
HG_HEADS = HG_WIDTH // HG_EXPAND
MOE_GROUPS = 4
MOE_PER_GROUP = 8
MOE_EXPERTS = MOE_GROUPS * MOE_PER_GROUP
MOE_TOPK = 2
MOE_HIDDEN = D_MODEL // 4
DN_ALPHA = (2 * DEPTH) ** 0.25
NORM_EPS = 1e-5

LANES = 128
VMEM_LIMIT = 52 * 1024 * 1024

WKV_CHUNK = 64
SSD_CHUNK = 128
GLA_CHUNK = 64
GLA_HEADS_PER_STEP = 16
SAMPLE_PAD = 8
MM_TM = 1088
MM_TN = 512
MOE_ROWS = 256
LN_ROWS = 272


def _bdot(a, b):
    return jnp.dot(a.astype(BF16), b.astype(BF16), preferred_element_type=F32)


def _bdot_nt(a, b):
    return lax.dot_general(a.astype(BF16), b.astype(BF16), (((1,), (1,)), ((), ())), preferred_element_type=F32)


def _bdot_tn(a, b):
    return lax.dot_general(a.astype(BF16), b.astype(BF16), (((0,), (0,)), ((), ())), preferred_element_type=F32)


def _tdot(sel, x):
    hi = x.astype(BF16)
    rest = x - hi.astype(F32)
    mid = rest.astype(BF16)
    lo = (rest - mid.astype(F32)).astype(BF16)
    d = lambda u: jnp.dot(sel, u, preferred_element_type=F32)
    return d(hi) + (d(mid) + d(lo))


def _hdot(a, b):
    return jnp.dot(a, b, precision=HIGHEST, preferred_element_type=F32)


def _iota2(shape, axis):
    return lax.broadcasted_iota(jnp.int32, shape, axis)


def _mm_kernel(x_ref, w_ref, o_ref, *, exact):
    w = w_ref[...].reshape(w_ref.shape[-2:])
    if exact:
        o_ref[...] = _hdot(x_ref[...], w)
    else:
        o_ref[...] = _bdot(x_ref[...], w)


def mm(x, w, layer=None, *, tm=MM_TM, tn=MM_TN, exact=False):
    m, k = x.shape
    n = w.shape[-1]
    tm = min(tm, m)
    tn = min(tn, n)
    if layer is None:
        w_spec = pl.BlockSpec((k, tn), lambda i, j: (0, j))
    else:
        w_spec = pl.BlockSpec((1, k, tn), lambda i, j: (layer, 0, j))
    return pl.pallas_call(
        functools.partial(_mm_kernel, exact=exact),
        grid=(pl.cdiv(m, tm), pl.cdiv(n, tn)),
        in_specs=[pl.BlockSpec((tm, k), lambda i, j: (i, 0)), w_spec],
        out_specs=pl.BlockSpec((tm, tn), lambda i, j: (i, j)),
        out_shape=jax.ShapeDtypeStruct((m, n), F32),
        compiler_params=pltpu.CompilerParams(dimension_semantics=("parallel", "parallel"),
                                             vmem_limit_bytes=VMEM_LIMIT),
        name="mm",
    )(x, w)


def _wkv7_kernel(r_ref, lw_ref, k_ref, v_ref, kk_ref, a_ref, g_ref, rk_ref, lg_ref, lb_ref, s0_ref,
                 *rest, chunk, pairs, fill_slot):
    y_ref, so_ref, s_scr = rest[-3:]
    c = pl.program_id(2)
    pair = s_scr.shape[1]
    head = pair // 2
    ps = range(pairs)
    same_head = (_iota2((pair, pair), 0) < head) == (_iota2((pair, pair), 1) < head)

    @pl.when(c == 0)
    def _():
        for p in ps:
            rows = s0_ref[0, 2 * p:2 * p + 2].reshape(pair, head)
            s_scr[p] = jnp.where(same_head, jnp.concatenate([rows, rows], axis=1), 0.0)

    row2 = _iota2((chunk, 2 * chunk), 0)
    lane2 = _iota2((chunk, 2 * chunk), 1)
    col2 = lane2 & (chunk - 1)
    strict = row2 > col2
    incl = row2 >= col2
    first = lane2 < chunk
    head0 = _iota2((chunk, pair), 1) < head
    tri = (_iota2((chunk, chunk), 0) >= _iota2((chunk, chunk), 1)).astype(BF16)
    lanes = lambda ref, p: ref[:, p * pair:(p + 1) * pair]
    rows2 = lambda t, u: jnp.concatenate([t, u], axis=0)
    stack = lambda t: rows2(jnp.where(head0, t, 0.0), jnp.where(head0, 0.0, t))
    blockdiag = lambda m: rows2(jnp.where(first, m, 0.0), jnp.where(first, 0.0, m))

    def head_sum(t):
        s0 = jnp.sum(jnp.where(head0, t, 0.0), axis=-1, keepdims=True)
        s1 = jnp.sum(jnp.where(head0, 0.0, t), axis=-1, keepdims=True)
        return jnp.where(head0, s0, s1)

    r, lw, k, v, kk, a = ([lanes(ref, p) for p in ps] for ref in (r_ref, lw_ref, k_ref, v_ref, kk_ref, a_ref))
    s = [s_scr[p] for p in ps]
    cum = [_tdot(tri, lw[p]) for p in ps]
    w_t = [jnp.exp(cum[p]) for p in ps]
    inv_w = [jnp.exp(-cum[p]) for p in ps]
    al = [-kk[p] * jnp.exp(cum[p] - lw[p]) for p in ps]
    be = [kk[p] * a[p] * inv_w[p] for p in ps]
    kh = [k[p] * inv_w[p] for p in ps]
    rt = [r[p] * w_t[p] for p in ps]
    lhs = [rows2(al[p], rt[p]) for p in ps]
    scores = [_bdot_nt(lhs[p], rows2(stack(be[p]), stack(kh[p]))) for p in ps]
    pw = [jnp.where(strict, scores[p][:chunk, :2 * chunk], 0.0) for p in ps]
    kmat = [jnp.where(strict, scores[p][:chunk, 2 * chunk:], 0.0) for p in ps]
    bmat = [jnp.where(incl, scores[p][chunk:, :2 * chunk], 0.0) for p in ps]
    vmat = [jnp.where(incl, scores[p][chunk:, 2 * chunk:], 0.0) for p in ps]
    zy = [_bdot_nt(lhs[p], s[p]) + _bdot(rows2(kmat[p], vmat[p]), stack(v[p])) for p in ps]
    m = 1
    level = lambda m: (((row2 & (2 * m - 1)) >= m) & ((col2 & (2 * m - 1)) < m)
                       & ((row2 & -(2 * m)) == (col2 & -(2 * m))))
    tinv = [jnp.where(row2 == col2, 1.0, jnp.where(level(1), pw[p], 0.0)) for p in ps]
    for _ in range(int(math.log2(chunk)) - 1):
        m *= 2
        part = [jnp.where(level(m), pw[p], 0.0) for p in ps]
        part = [_bdot(part[p], blockdiag(tinv[p])) for p in ps]
        tinv = [tinv[p] + _bdot(tinv[p], blockdiag(part[p])) for p in ps]
    u = [_bdot(tinv[p], stack(zy[p][:chunk])) for p in ps]
    y = [zy[p][chunk:] + _bdot(bmat[p], stack(u[p])) for p in ps]
    w_c = [w_t[p][chunk - 1:chunk, :] for p in ps]
    s_new = [s[p] * w_c[p] + jnp.where(same_head, _bdot_tn(rows2(u[p], v[p]), rows2(be[p] * w_c[p], kh[p] * w_c[p])), 0.0)
             for p in ps]
    mean = [head_sum(y[p]) / head for p in ps]
    dev = [y[p] - mean[p] for p in ps]
    var = [head_sum(jnp.square(dev[p])) / head for p in ps]
    bonus = [head_sum(r[p] * k[p] * lanes(rk_ref, p)) * v[p] for p in ps]
    out = [(dev[p] * lax.rsqrt(var[p] + RW_GN_EPS) * lanes(lg_ref, p) + lanes(lb_ref, p) + bonus[p]) * lanes(g_ref, p)
           for p in ps]
    for p in ps:
        y_ref[:, p * pair:(p + 1) * pair] = out[p]
        s_scr[p] = s_new[p]

    @pl.when(c == pl.num_programs(2) - 1)
    def _():
        out = _state_view(so_ref, fill_slot)
        for p in ps:
            folded = s_scr[p][:, :head] + s_scr[p][:, head:]
            out[0, 2 * p:2 * p + 2] = folded.reshape(2, head, head)


def _state_spec(block, index, layer):
    if layer is None:
        return pl.BlockSpec(block, index)
    return pl.BlockSpec((None,) + block, lambda *g: (layer,) + index(*g))


def _state_out(block, index, shape, n_inputs, stack):
    if stack is None:
        return pl.BlockSpec(block, index), jax.ShapeDtypeStruct(shape, F32), [], [], {}, None
    n_layers, slot, prev = stack
    full = jax.ShapeDtypeStruct((n_layers,) + shape, F32)
    if prev is None:
        return pl.BlockSpec((n_layers,) + block, lambda *g: (0,) + index(*g)), full, [], [], {}, slot
    spec = pl.BlockSpec((None,) + block, lambda *g: (slot,) + index(*g))
    return spec, full, [pl.BlockSpec(memory_space=pl.ANY)], [prev], {n_inputs: 1}, None


def _state_view(so_ref, fill_slot):
    if fill_slot is None:
        return so_ref
    for other in range(so_ref.shape[0]):
        if other != fill_slot:
            so_ref[other] = jnp.zeros(so_ref.shape[1:], so_ref.dtype)
    return so_ref.at[fill_slot]


def wkv7(r, lw, k, v, kk, a, g, r_k, lnx_g, lnx_b, s0, layer=None, stack=None, *, n_seq, chunk, n_chunks, pairs):
    h, n = s0.shape[-3:-1]
    width = h * n
    seq = pl.BlockSpec((chunk, 2 * pairs * n), lambda i, j, c: (i * n_chunks + c, j))
    vec = pl.BlockSpec((1, 2 * pairs * n), lambda i, j, c: (0, j))
    st_block, st_index = (1, 2 * pairs, n, n), lambda i, j, c: (i, j, 0, 0)
    args = [r, lw, k, v, kk, a, g, r_k.reshape(1, width), lnx_g.reshape(1, width), lnx_b.reshape(1, width), s0]
    so_spec, so_shape, more_specs, more_args, aliases, fill = _state_out(st_block, st_index, (n_seq, h, n, n),
                                                                         len(args), stack)
    return pl.pallas_call(
        functools.partial(_wkv7_kernel, chunk=chunk, pairs=pairs, fill_slot=fill),
        grid=(n_seq, h // (2 * pairs), n_chunks),
        in_specs=[seq] * 7 + [vec] * 3 + [_state_spec(st_block, st_index, layer)] + more_specs,
        out_specs=[seq, so_spec],
        out_shape=[jax.ShapeDtypeStruct((n_seq * n_chunks * chunk, width), F32), so_shape],
        scratch_shapes=[pltpu.VMEM((pairs, 2 * n, 2 * n), F32)],
        input_output_aliases=aliases,
        compiler_params=pltpu.CompilerParams(dimension_semantics=("parallel", "parallel", "arbitrary"),
                                             vmem_limit_bytes=VMEM_LIMIT),
        name="wkv7",
    )(*args, *more_args)


def _ssd_kernel(x_ref, z_ref, dt_ref, da_ref, dat_ref, b_ref, c_ref, e_ref, dsk_ref, nw_ref, s0_ref,
                *rest, chunk, fill_slot):
    y_ref, so_ref, s_scr = rest[-3:]
    c = pl.program_id(1)
    pairs, pair, n_state = s_scr.shape
    head = pair // 2
    pairs_per_group = pairs * n_state // b_ref.shape[-1]

    @pl.when(c == 0)
    def _():
        s_scr[...] = s0_ref[0].reshape(pairs, pair, n_state)

    row = _iota2((chunk, chunk), 0)
    col = _iota2((chunk, chunk), 1)
    incl = row >= col
    cum_all = _hdot(incl.astype(F32), da_ref[...])
    cum_t = _hdot(dat_ref[0], (row <= col).astype(F32))
    expand = e_ref[...]
    cum_lanes = _hdot(cum_all, expand)
    dt_lanes = _hdot(dt_ref[...], expand)
    x = x_ref[...]
    xdt = x * dt_lanes
    xdt_end = xdt * jnp.exp(cum_lanes[chunk - 1:chunk] - cum_lanes)
    ecum = jnp.exp(cum_lanes)
    bm = b_ref[...]
    cm = c_ref[...]
    group = lambda t, p: t[:, (p // pairs_per_group) * n_state:(p // pairs_per_group + 1) * n_state]
    cb = [_bdot_nt(cm[:, g * n_state:(g + 1) * n_state], bm[:, g * n_state:(g + 1) * n_state])
          for g in range(pairs // pairs_per_group)]

    def decay_scores(h):
        seg = cum_all[:, h:h + 1] - cum_t[h:h + 1, :]
        return cb[h // (2 * pairs_per_group)] * jnp.where(incl, jnp.exp(jnp.minimum(seg, 0.0)), 0.0)

    ps = range(pairs)
    left = _iota2((chunk, pair), 1) < head
    top = _iota2((pair, n_state), 0) < head
    s = [s_scr[p] for p in ps]
    m0 = [decay_scores(2 * p) for p in ps]
    m1 = [decay_scores(2 * p + 1) for p in ps]
    xp = [xdt[:, p * pair:(p + 1) * pair] for p in ps]
    if chunk % LANES == 0:
        intra = [_bdot(jnp.concatenate([m0[p], m1[p]], axis=1),
                       jnp.concatenate([jnp.where(left, xp[p], 0.0), jnp.where(left, 0.0, xp[p])], axis=0)) for p in ps]
    else:
        intra = [_bdot(m0[p], jnp.where(left, xp[p], 0.0)) + _bdot(m1[p], jnp.where(left, 0.0, xp[p])) for p in ps]
    inter = [_bdot_nt(group(cm, p), s[p]) * ecum[:, p * pair:(p + 1) * pair] for p in ps]
    last = jnp.exp(cum_all[chunk - 1:chunk, :])
    dec = [jnp.where(top, last[:, 2 * p:2 * p + 1], last[:, 2 * p + 1:2 * p + 2]) for p in ps]
    s_new = [s[p] * dec[p] + _bdot_tn(xdt_end[:, p * pair:(p + 1) * pair], group(bm, p)) for p in ps]
    lanes = lambda t, p: t[:, p * pair:(p + 1) * pair]
    gated = [(intra[p] + inter[p] + lanes(dsk_ref, p) * lanes(x, p)) * jax.nn.silu(lanes(z_ref, p)) for p in ps]
    sq = [jnp.sum(jnp.square(gated[p]), axis=-1, keepdims=True) for p in ps]
    inv_rms = []
    for g in range(pairs // pairs_per_group):
        total = sum(sq[g * pairs_per_group:(g + 1) * pairs_per_group])
        inv_rms.append(lax.rsqrt(total / (pairs_per_group * pair) + NORM_EPS))
    for p in ps:
        y_ref[:, p * pair:(p + 1) * pair] = gated[p] * inv_rms[p // pairs_per_group] * lanes(nw_ref, p)
        s_scr[p] = s_new[p]

    @pl.when(c == pl.num_programs(1) - 1)
    def _():
        _state_view(so_ref, fill_slot)[0] = s_scr[...].reshape(2 * pairs, head, n_state)


def ssd(xs, z, dt, da, bm, cm, d_skip, norm_w, s0, layer=None, stack=None, *, n_seq, chunk, n_chunks):
    width = xs.shape[-1]
    n_heads = dt.shape[-1]
    head = width // n_heads
    n_state = s0.shape[-1]
    rows = n_seq * n_chunks * chunk
    dat = jnp.swapaxes(da[:rows].reshape(n_seq * n_chunks, chunk, n_heads), 1, 2)
    expand = (jnp.arange(width)[None, :] // head == jnp.arange(n_heads)[:, None]).astype(F32)
    seq = lambda w: pl.BlockSpec((chunk, w), lambda i, c: (i * n_chunks + c, 0))
    const = lambda shape: pl.BlockSpec(shape, lambda i, c: (0,) * len(shape))
    st_block, st_index = (1, n_heads, head, n_state), lambda i, c: (i, 0, 0, 0)
    args = [xs, z, dt, da, dat, bm, cm, expand, jnp.repeat(d_skip, head).reshape(1, width), norm_w.reshape(1, width), s0]
    so_spec, so_shape, more_specs, more_args, aliases, fill = _state_out(st_block, st_index, (n_seq,) + s0.shape[-3:],
                                                                         len(args), stack)
    return pl.pallas_call(
        functools.partial(_ssd_kernel, chunk=chunk, fill_slot=fill),
        grid=(n_seq, n_chunks),
        in_specs=[seq(width), seq(width), seq(n_heads), seq(n_heads),
                  pl.BlockSpec((1, n_heads, chunk), lambda i, c: (i * n_chunks + c, 0, 0)),
                  seq(bm.shape[-1]), seq(cm.shape[-1]), const(expand.shape), const((1, width)), const((1, width)),
                  _state_spec(st_block, st_index, layer)] + more_specs,
        out_specs=[seq(width), so_spec],
        out_shape=[jax.ShapeDtypeStruct((rows, width), F32), so_shape],
        scratch_shapes=[pltpu.VMEM((n_heads // 2, 2 * head, n_state), F32)],
        input_output_aliases=aliases,
        compiler_params=pltpu.CompilerParams(dimension_semantics=("parallel", "arbitrary"),
                                             vmem_limit_bytes=VMEM_LIMIT),
        name="ssd",
    )(*args, *more_args)


def _gla_levels(chunk):
    halves = []
    m = chunk // 2
    while m >= 1:
        halves.append(m)
        m //= 2
    return halves


def _level_ref(cum, m):
    rows, n = cum.shape
    if m >= 8:
        c3 = cum.reshape(rows // (2 * m), 2 * m, n)
        return jnp.broadcast_to(c3[:, m - 1:m, :], c3.shape).reshape(rows, n)
    c3 = cum.reshape(rows // 8, 8, n)
    sub = lax.broadcasted_iota(jnp.int32, c3.shape, 1)
    ref = jnp.broadcast_to(c3[:, m - 1:m, :], c3.shape)
    for b0 in range(2 * m, 8, 2 * m):
        ref = jnp.where(sub >= b0, jnp.broadcast_to(c3[:, b0 + m - 1:b0 + m, :], c3.shape), ref)
    return ref.reshape(rows, n)


def _gla_kernel(q_ref, f_ref, i_ref, g_ref, lo_ref, nw_ref, s0_ref, *rest, chunk, heads, valid, fill_slot):
    y_ref, so_ref, s_scr = rest[-3:]
    c = pl.program_id(2)
    kd = s_scr.shape[1]

    @pl.when(c == 0)
    def _():
        s_scr[...] = s0_ref[0]

    row = _iota2((chunk, chunk), 0)
    col = _iota2((chunk, chunk), 1)
    trow = _iota2((chunk, 1), 0)
    tri = (row >= col).astype(BF16)
    live = trow < valid
    hs = range(heads)
    lanes = lambda ref, h: ref[:, h * kd:(h + 1) * kd]
    lower = [lanes(lo_ref, h) for h in hs]
    fg = [lower[h] + (1.0 - lower[h]) * jax.nn.sigmoid(lanes(f_ref, h)) for h in hs]
    q = [jax.nn.silu(lanes(q_ref, h)) for h in hs]
    k = [1.0 - fg[h] for h in hs]
    lf = [jnp.log(fg[h]) for h in hs]
    if valid < chunk:
        q = [jnp.where(live, q[h], 0.0) for h in hs]
        k = [jnp.where(live, k[h], 0.0) for h in hs]
        lf = [jnp.where(live, lf[h], 0.0) for h in hs]
    v = [lanes(i_ref, h) for h in hs]
    s = [s_scr[h] for h in hs]
    cum = [_tdot(tri, lf[h]) for h in hs]
    att = [jnp.zeros((chunk, chunk), F32) for _ in hs]
    for m in _gla_levels(chunk):
        second = (trow & (2 * m - 1)) >= m
        same = (row & -(2 * m)) == (col & -(2 * m))
        ref = [_level_ref(cum[h], m) for h in hs]
        ql = [jnp.where(second, q[h] * jnp.exp(jnp.where(second, jnp.minimum(cum[h] - ref[h], 0.0), 0.0)), 0.0)
              for h in hs]
        kl = [jnp.where(second, 0.0, k[h] * jnp.exp(jnp.where(second, 0.0, jnp.minimum(ref[h] - cum[h], 0.0))))
              for h in hs]
        att = [att[h] + jnp.where(same, _bdot_nt(ql[h], kl[h]), 0.0) for h in hs]
    diag = [jnp.sum(q[h] * k[h], axis=-1, keepdims=True) for h in hs]
    o = [_bdot(att[h], v[h]) + diag[h] * v[h] + _bdot(q[h] * jnp.exp(cum[h]), s[h]) for h in hs]
    tot = [cum[h][chunk - 1:chunk, :] for h in hs]
    dec = [jnp.broadcast_to(jnp.exp(tot[h]), (8, kd)).T[:, :1] for h in hs]
    s_new = [s[h] * dec[h] + _bdot_tn(k[h] * jnp.exp(tot[h] - cum[h]), v[h]) for h in hs]
    y = [o[h] * lax.rsqrt(jnp.mean(jnp.square(o[h]), -1, keepdims=True) + NORM_EPS) * lanes(nw_ref, h)
         * jax.nn.silu(lanes(g_ref, h)) for h in hs]
    for h in hs:
        y_ref[:, h * kd:(h + 1) * kd] = y[h]
        s_scr[h] = s_new[h]

    @pl.when(c == pl.num_programs(2) - 1)
    def _():
        _state_view(so_ref, fill_slot)[0] = s_scr[...]


def gla(p, lower, norm_w, s0, layer=None, stack=None, *, n_seq, chunk, n_chunks, heads, valid):
    h, kd, vd = s0.shape[-3:]
    width = h * kd
    blocks = h // heads
    seg = lambda which: pl.BlockSpec((chunk, heads * kd), lambda i, j, c: (i * n_chunks + c, which * blocks + j))
    vec = pl.BlockSpec((1, heads * kd), lambda i, j, c: (0, j))
    st_block, st_index = (1, heads, kd, vd), lambda i, j, c: (i, j, 0, 0)
    args = [p, p, p, p, lower.reshape(1, width), norm_w.reshape(1, width), s0]
    so_spec, so_shape, more_specs, more_args, aliases, fill = _state_out(st_block, st_index, (n_seq, h, kd, vd),
                                                                         len(args), stack)
    return pl.pallas_call(
        functools.partial(_gla_kernel, chunk=chunk, heads=heads, valid=valid, fill_slot=fill),
        grid=(n_seq, blocks, n_chunks),
        in_specs=[seg(0), seg(1), seg(2), seg(3), vec, vec, _state_spec(st_block, st_index, layer)] + more_specs,
        out_specs=[seg(0), so_spec],
        out_shape=[jax.ShapeDtypeStruct((n_seq * n_chunks * chunk, width), F32), so_shape],
        scratch_shapes=[pltpu.VMEM((heads, kd, vd), F32)],
        input_output_aliases=aliases,
        compiler_params=pltpu.CompilerParams(dimension_semantics=("parallel", "parallel", "arbitrary"),
                                             vmem_limit_bytes=VMEM_LIMIT),
        name="gla",
    )(*args, *more_args)


def _moe_kernel(be_ref, used_ref, xl_ref, xr_ref, wg_ref, wu_ref, wd_ref, o_ref):
    i = pl.program_id(0)

    @pl.when(used_ref[i] > 0)
    def _():
        x = jnp.concatenate([xl_ref[...], xr_ref[...]], axis=-1)
        hg = _bdot(x, wg_ref[0, 0])
        hu = _bdot(x, wu_ref[0, 0])
        hid = hg * jax.nn.sigmoid(hg) * hu
        o_ref[...] = _bdot(hid, wd_ref[0, 0])

    @pl.when(used_ref[i] == 0)
    def _():
        o_ref[...] = jnp.zeros_like(o_ref)


def moe_experts(block_e, block_used, x_left, x_right, w_gate, w_up, w_down, layer, *, rows):
    n_slots = x_left.shape[0]
    d, hid = w_gate.shape[-2:]
    grid_spec = pltpu.PrefetchScalarGridSpec(
        num_scalar_prefetch=2,
        grid=(n_slots // rows,),
        in_specs=[
            pl.BlockSpec((rows, d // 2), lambda i, be, used: (i, 0)),
            pl.BlockSpec((rows, d // 2), lambda i, be, used: (i, 0)),
            pl.BlockSpec((1, 1, d, hid), lambda i, be, used: (layer, be[i], 0, 0)),
            pl.BlockSpec((1, 1, d, hid), lambda i, be, used: (layer, be[i], 0, 0)),
            pl.BlockSpec((1, 1, hid, d), lambda i, be, used: (layer, be[i], 0, 0)),
        ],
        out_specs=pl.BlockSpec((rows, d), lambda i, be, used: (i, 0)),
    )
    return pl.pallas_call(
        _moe_kernel,
        grid_spec=grid_spec,
        out_shape=jax.ShapeDtypeStruct((n_slots, d), F32),
        compiler_params=pltpu.CompilerParams(dimension_semantics=("arbitrary",), vmem_limit_bytes=VMEM_LIMIT),
        name="moe_experts",
    )(block_e, block_used, x_left, x_right, w_gate, w_up, w_down)


def _first_max(p):
    idx = lax.broadcasted_iota(jnp.int32, p.shape, p.ndim - 1)
    top = jnp.max(p, axis=-1, keepdims=True)
    return top, jnp.min(jnp.where(p == top, idx, p.shape[-1]), axis=-1, keepdims=True)


def _prefix_counts(onehot):
    m, e = onehot.shape
    blk = LANES
    oh = onehot.reshape(m // blk, blk, e)
    r = jnp.arange(blk)
    inside = jnp.einsum('ts,bse->bte', (r[:, None] > r[None, :]).astype(F32), oh, precision=HIGHEST)
    totals = jnp.sum(oh, axis=1)
    b = jnp.arange(m // blk)
    before = jnp.einsum('ab,be->ae', (b[:, None] > b[None, :]).astype(F32), totals, precision=HIGHEST)
    return (inside + before[:, None, :]).reshape(m, e), jnp.sum(totals, axis=0)


def hier_moe(x, x_halves, wr_grp, br_grp, wr_exp, br_exp, w_gate, w_up, w_down, layer):
    t, d = x.shape
    w_route = jnp.concatenate([wr_grp, jnp.moveaxis(wr_exp, 0, 1).reshape(d, MOE_EXPERTS)], axis=1)
    logits = mm(x, w_route, exact=True)
    grp_p, grp_i = _first_max(jax.nn.softmax(logits[:, :MOE_GROUPS] + br_grp, axis=-1))
    exp_all = logits[:, MOE_GROUPS:].reshape(t, MOE_GROUPS, MOE_PER_GROUP) + br_exp
    in_grp = lax.broadcasted_iota(jnp.int32, (t, MOE_GROUPS, 1), 1) == grp_i[:, :, None]
    exp_prob = jax.nn.softmax(jnp.sum(jnp.where(in_grp, exp_all, 0.0), axis=1), axis=-1)
    p1, i1 = _first_max(exp_prob)
    taken = lax.broadcasted_iota(jnp.int32, exp_prob.shape, 1) == i1
    p2, i2 = _first_max(jnp.where(taken, -jnp.inf, exp_prob))
    top_p = jnp.concatenate([p1, p2], axis=1)
    gates = grp_p * top_p / jnp.sum(top_p, -1, keepdims=True)
    expert = grp_i * MOE_PER_GROUP + jnp.concatenate([i1, i2], axis=1)
    m = t * MOE_TOPK
    onehot = (expert.reshape(m, 1) == jnp.arange(MOE_EXPERTS)[None, :]).astype(F32)
    ranks, counts = _prefix_counts(onehot)
    padded = jnp.ceil(counts / MOE_ROWS) * MOE_ROWS
    ends = jnp.sum(jnp.where(jnp.arange(MOE_EXPERTS)[:, None] <= jnp.arange(MOE_EXPERTS)[None, :],
                             padded[:, None], 0.0), axis=0)
    slot = jnp.sum(onehot * (ranks + (ends - padded)[None, :]), axis=1).astype(jnp.int32)
    n_blocks = -(-m // MOE_ROWS) + MOE_EXPERTS
    n_slots = n_blocks * MOE_ROWS
    flat_t = lax.broadcasted_iota(jnp.int32, (t, MOE_TOPK), 0).reshape(m)
    slot_tok = (jnp.arange(n_slots, dtype=jnp.int32) % t).at[slot].set(flat_t)
    block_start = jnp.arange(n_blocks, dtype=jnp.int32) * MOE_ROWS
    ends_i = ends.astype(jnp.int32)
    block_e = jnp.minimum(jnp.sum((block_start[:, None] >= ends_i[None, :]).astype(jnp.int32), axis=1),
                          MOE_EXPERTS - 1)
    block_used = (block_start < ends_i[-1]).astype(jnp.int32)
    ys = moe_experts(block_e, block_used, x_halves[0][slot_tok], x_halves[1][slot_tok], w_gate, w_up, w_down, layer,
                     rows=MOE_ROWS)
    slot = slot.reshape(t, MOE_TOPK)
    return ys[slot[:, 0]], ys[slot[:, 1]], gates


def _ln_kernel(*refs, n_terms, gated, extra):
    x_ref = refs[0]
    terms = refs[1:1 + n_terms]
    pos = 1 + n_terms
    gate_ref = refs[pos] if gated else None
    pos += int(gated)
    g_ref, b_ref = refs[pos:pos + 2]
    outs = refs[pos + 2:]
    h = DN_ALPHA * x_ref[...]
    for i, term in enumerate(terms):
        h = h + (term[...] * gate_ref[:, i:i + 1] if gated else term[...])
    mu = jnp.mean(h, -1, keepdims=True)
    dev = h - mu
    var = jnp.mean(jnp.square(dev), -1, keepdims=True)
    y = dev * lax.rsqrt(var + NORM_EPS) * g_ref[...] + b_ref[...]
    outs[0][...] = y
    if extra == "bf16":
        outs[1][...] = y.astype(BF16)
    elif extra == "halves":
        half = y.shape[-1] // 2
        outs[1][...] = y[:, :half]
        outs[2][...] = y[:, half:]


def residual_layer_norm(x, terms, gates, g, b, *, extra):
    t, d = x.shape
    rows = pl.BlockSpec((LN_ROWS, d), lambda i: (i, 0))
    vec = pl.BlockSpec((1, d), lambda i: (0, 0))
    gate_specs = [] if gates is None else [pl.BlockSpec((LN_ROWS, gates.shape[1]), lambda i: (i, 0))]
    gate_args = [] if gates is None else [gates]
    out_shape = [jax.ShapeDtypeStruct((t, d), F32)]
    out_specs = [rows]
    if extra == "bf16":
        out_shape.append(jax.ShapeDtypeStruct((t, d), BF16))
        out_specs.append(rows)
    elif extra == "halves":
        out_shape += [jax.ShapeDtypeStruct((t, d // 2), F32)] * 2
        out_specs += [pl.BlockSpec((LN_ROWS, d // 2), lambda i: (i, 0))] * 2
    return pl.pallas_call(
        functools.partial(_ln_kernel, n_terms=len(terms), gated=gates is not None, extra=extra),
        grid=(t // LN_ROWS,),
        in_specs=[rows] * (1 + len(terms)) + gate_specs + [vec, vec],
        out_specs=out_specs,
        out_shape=out_shape,
        compiler_params=pltpu.CompilerParams(dimension_semantics=("parallel",), vmem_limit_bytes=VMEM_LIMIT),
        name="residual_layer_norm",
    )(x, *terms, *gate_args, g.reshape(1, d), b.reshape(1, d))


def _pad_len(t, length):
    return jnp.pad(t, [(0, 0), (0, length - t.shape[1])] + [(0, 0)] * (t.ndim - 2))


def _shift_rows(p, prev, b, l):
    seq = p.reshape(b, l, p.shape[-1])
    return jnp.concatenate([prev[:, None, :], seq[:, :-1]], axis=1).reshape(p.shape), seq[:, -1]


def even_mixer(x, lens, e, shift0, wkv0, conv0, ssm0, v_first, vres, w_in, w_out, mu, w0, w_up, a0, a_up, g_up,
               k_k, k_a, r_k, lnx_g, lnx_b, conv_w, conv_b, dt_bias, a_log, d_skip, norm_w):
    (bp, lp, _), (bs, ls, _) = lens
    np_rows = bp * lp
    t = x.shape[0]
    p = mm(x, w_in, e)
    p_rw, p_ssm = p[:, :RW_PROJ], p[:, RW_PROJ:]
    prev_p, shift_p = _shift_rows(p_rw[:np_rows], shift0[0], bp, lp)
    prev_s, shift_s = _shift_rows(p_rw[np_rows:], shift0[1], bs, ls)
    pm = p_rw + (jnp.concatenate([prev_p, prev_s], axis=0) - p_rw) * mu
    cuts = [RW_WIDTH, 2 * RW_WIDTH, 3 * RW_WIDTH, 3 * RW_WIDTH + RW_LORA_W, 3 * RW_WIDTH + RW_LORA_W + RW_LORA_A]
    r, k, v, wd, ad, gd = jnp.split(pm, cuts, axis=-1)
    log_w = -jax.nn.softplus(-(w0 + mm(jnp.tanh(wd), w_up))) - 0.5
    lw = -jnp.exp(log_w)
    if vres is None:
        v_first = v
    else:
        v0, v_down, v_up = vres
        v = v + (v_first - v) * jax.nn.sigmoid(v0 + mm(mm(v, v_down), v_up))
    a = jax.nn.sigmoid(a0 + mm(ad, a_up))
    g = mm(jax.nn.sigmoid(gd), g_up)
    heads = lambda u: u.reshape(t, RW_HEADS, RW_HEAD)
    kk = heads(k * k_k)
    kk = kk * lax.rsqrt(jnp.maximum(jnp.sum(jnp.square(kk), -1, keepdims=True), 1e-24))
    k = k * (1.0 + (a - 1.0) * k_a)
    seqs = [r, lw, k, v, kk.reshape(t, RW_WIDTH), a, g]
    chunk_p = min(WKV_CHUNK, lp)
    o_p, wkv_p = wkv7(*seqs, r_k, lnx_g, lnx_b, *wkv0[0], n_seq=bp, chunk=chunk_p, n_chunks=lp // chunk_p,
                      pairs=RW_HEADS // 2)
    padded = [_pad_len(u[np_rows:].reshape(bs, ls, RW_WIDTH), SAMPLE_PAD).reshape(bs * SAMPLE_PAD, RW_WIDTH)
              for u in seqs]
    o_s, wkv_s = wkv7(*padded, r_k, lnx_g, lnx_b, *wkv0[1], n_seq=bs, chunk=SAMPLE_PAD, n_chunks=1,
                      pairs=RW_HEADS // 2)
    o_rw = jnp.concatenate([o_p, o_s.reshape(bs, SAMPLE_PAD, RW_WIDTH)[:, :ls].reshape(bs * ls, RW_WIDTH)], axis=0)
    z = p_ssm[:, :SSM_WIDTH]
    xbc_raw = p_ssm[:, SSM_WIDTH:SSM_WIDTH + SSM_CONV_CH]
    dt_raw = p_ssm[:, SSM_WIDTH + SSM_CONV_CH:]

    def conv(u, buf, b, l):
        full = jnp.concatenate([buf, u.reshape(b, l, SSM_CONV_CH)], axis=1)
        y = sum(full[:, j:j + l] * conv_w[j] for j in range(SSM_CONV)) + conv_b
        return y.reshape(b * l, SSM_CONV_CH), full[:, l:]

    xbc_p, conv_p = conv(xbc_raw[:np_rows], conv0[0], bp, lp)
    xbc_s, conv_s = conv(xbc_raw[np_rows:], conv0[1], bs, ls)
    xbc = jax.nn.silu(jnp.concatenate([xbc_p, xbc_s], axis=0))
    gn = SSM_GROUPS * SSM_STATE
    xs, bm, cm = xbc[:, :SSM_WIDTH], xbc[:, SSM_WIDTH:SSM_WIDTH + gn], xbc[:, SSM_WIDTH + gn:]
    dt = jax.nn.softplus(dt_raw + dt_bias)
    seqs = [xs, z, dt, dt * -jnp.exp(a_log), bm, cm]
    chunk_p = min(SSD_CHUNK, lp)
    y_p, ssm_p = ssd(*seqs, d_skip, norm_w, *ssm0[0], n_seq=bp, chunk=chunk_p, n_chunks=lp // chunk_p)
    padded = [_pad_len(u[np_rows:].reshape(bs, ls, u.shape[-1]), SAMPLE_PAD).reshape(bs * SAMPLE_PAD, u.shape[-1])
              for u in seqs]
    y_s, ssm_s = ssd(*padded, d_skip, norm_w, *ssm0[1], n_seq=bs, chunk=SAMPLE_PAD, n_chunks=1)
    y = jnp.concatenate([y_p, y_s.reshape(bs, SAMPLE_PAD, SSM_WIDTH)[:, :ls].reshape(bs * ls, SSM_WIDTH)], axis=0)
    mix = mm(jnp.concatenate([o_rw, y], axis=-1).astype(BF16), w_out, e, tm=MM_TM // 2)
    return mix, v_first, (shift_p, shift_s), (wkv_p, wkv_s), (conv_p, conv_s), (ssm_p, ssm_s)


def hgrn2_mixer(x, lens, o, s0, w_in, w_out, lower, norm_w):
    (bp, lp, _), (bs, ls, _) = lens
    p = mm(x, w_in, o)
    y_p, hg_p = gla(p, lower, norm_w, *s0[0], n_seq=bp, chunk=GLA_CHUNK, n_chunks=lp // GLA_CHUNK,
                    heads=GLA_HEADS_PER_STEP, valid=GLA_CHUNK)
    p_s = _pad_len(p[bp * lp:].reshape(bs, ls, p.shape[-1]), SAMPLE_PAD).reshape(bs * SAMPLE_PAD, p.shape[-1])
    y_s, hg_s = gla(p_s, lower, norm_w, *s0[1], n_seq=bs, chunk=SAMPLE_PAD, n_chunks=1, heads=HG_HEADS, valid=ls)
    y_s = y_s.reshape(bs, SAMPLE_PAD, -1)[:, :ls].reshape(bs * ls, -1)
    return mm(jnp.concatenate([y_p, y_s], axis=0).astype(BF16), w_out, o), (hg_p, hg_s)


def kernel(x_prompt, x_sample, state_rwkv_shift, state_rwkv_wkv, state_ssm_conv, state_ssm, state_hgrn,
           ev_w_in, ev_w_out, rw_mu, rw_w0, rw_w_up, rw_a0, rw_a_up, rw_g_up, rw_k_k, rw_k_a, rw_r_k,
           rw_lnx_g, rw_lnx_b, rw_v0, rw_v_down, rw_v_up, ssm_conv_w, ssm_conv_b, ssm_dt_bias, ssm_a_log,
           ssm_d, ssm_norm_w, od_w_in, od_w_out, hg_lower_bounds, hg_norm_w, ln1_g, ln1_b, ln2_g, ln2_b,
           moe_wr_grp, moe_br_grp, moe_wr_exp, moe_br_exp, moe_w_gate, moe_w_up, moe_w_down):
    lb_soft = jax.nn.softmax(hg_lower_bounds, axis=0)
    lower = jnp.cumsum(lb_soft, axis=0) - lb_soft[0]
    bp, lp, d = x_prompt.shape
    bs, ls, _ = x_sample.shape
    lens = ((bp, lp, None), (bs, ls, None))
    x = jnp.concatenate([x_prompt.reshape(bp * lp, d), x_sample.reshape(bs * ls, d)], axis=0)
    zeros_p = lambda s: jnp.zeros((bp,) + s.shape[2:], s.dtype)
    x_bf16 = x.astype(BF16)
    v_first = None
    shifts, wkvs, convs, ssms, hgs = [], [], [], [], []
    wkv_all = ssm_all = hg_all = None
    for layer in range(DEPTH):
        if layer % 2 == 0:
            e = layer // 2
            n_even = state_rwkv_wkv.shape[0]
            vres = None if e == 0 else (rw_v0[e - 1], rw_v_down[e - 1], rw_v_up[e - 1])
            mix, v_first, s_sh, s_wkv, s_cv, s_ss = even_mixer(
                x_bf16, lens, e, (zeros_p(state_rwkv_shift), state_rwkv_shift[e]),
                ((zeros_p(state_rwkv_wkv), None), (state_rwkv_wkv, e, (n_even, e, wkv_all))),
                (zeros_p(state_ssm_conv), state_ssm_conv[e]),
                ((zeros_p(state_ssm), None), (state_ssm, e, (n_even, e, ssm_all))), v_first, vres,
                ev_w_in, ev_w_out, rw_mu[e], rw_w0[e], rw_w_up[e], rw_a0[e], rw_a_up[e], rw_g_up[e], rw_k_k[e],
                rw_k_a[e], rw_r_k[e], rw_lnx_g[e], rw_lnx_b[e], ssm_conv_w[e], ssm_conv_b[e], ssm_dt_bias[e],
                ssm_a_log[e], ssm_d[e], ssm_norm_w[e])
            shifts.append(s_sh)
            convs.append(s_cv)
            wkvs.append(s_wkv[0])
            ssms.append(s_ss[0])
            wkv_all, ssm_all = s_wkv[1], s_ss[1]
        else:
            o = layer // 2
            mix, s_hg = hgrn2_mixer(x_bf16, lens, o,
                                    ((zeros_p(state_hgrn), None), (state_hgrn, o, (state_hgrn.shape[0], o, hg_all))),
                                    od_w_in, od_w_out, lower[layer], hg_norm_w[o])
            hgs.append(s_hg[0])
            hg_all = s_hg[1]
        x, x_lo, x_hi = residual_layer_norm(x, [mix], None, ln1_g[layer], ln1_b[layer], extra="halves")
        y0, y1, gates = hier_moe(x, (x_lo, x_hi), moe_wr_grp[layer], moe_br_grp[layer], moe_wr_exp[layer], moe_br_exp[layer],
                                 moe_w_gate, moe_w_up, moe_w_down, layer)
        x, x_bf16 = residual_layer_norm(x, [y0, y1], gates, ln2_g[layer], ln2_b[layer], extra="bf16")
    stack = lambda pairs, j: jnp.stack([p[j] for p in pairs])
    np_rows = bp * lp
    return (x[:np_rows].reshape(bp, lp, d), x[np_rows:].reshape(bs, ls, d),
            stack(shifts, 0), jnp.stack(wkvs), stack(convs, 0), jnp.stack(ssms), jnp.stack(hgs),
            stack(shifts, 1), wkv_all, stack(convs, 1), ssm_all, hg_all)
```

```python
import functools
import math

import jax
import jax.numpy as jnp
from jax import lax
from jax.experimental import pallas as pl
from jax.experimental.pallas import tpu as pltpu

F32 = jnp.float32
BF16 = jnp.bfloat16
HIGHEST = lax.Precision.HIGHEST

D_MODEL = 2048
DEPTH = 4
RW_HEAD = 64
RW_WIDTH = D_MODEL // 2
RW_HEADS = RW_WIDTH // RW_HEAD
RW_LORA_W = 64
RW_LORA_A = 64
RW_LORA_G = 160
RW_PROJ = 3 * RW_WIDTH + RW_LORA_W + RW_LORA_A + RW_LORA_G
RW_GN_EPS = 64e-5
SSM_WIDTH = D_MODEL
SSM_HEAD = 64
SSM_HEADS = SSM_WIDTH // SSM_HEAD
SSM_STATE = 128
SSM_GROUPS = 4
SSM_CONV = 4
SSM_CONV_CH = SSM_WIDTH + 2 * SSM_GROUPS * SSM_STATE
HG_WIDTH = D_MODEL
HG_EXPAND = 128
HG_HEADS = HG_WIDTH // HG_EXPAND
MOE_GROUPS = 4
MOE_PER_GROUP = 8
MOE_EXPERTS = MOE_GROUPS * MOE_PER_GROUP
MOE_TOPK = 2
MOE_HIDDEN = D_MODEL // 4
DN_ALPHA = (2 * DEPTH) ** 0.25
NORM_EPS = 1e-5

LANES = 128
VMEM_LIMIT = 52 * 1024 * 1024

WKV_CHUNK = 64
SSD_CHUNK = 128
GLA_CHUNK = 64
GLA_HEADS_PER_STEP = 16
SAMPLE_PAD = 8
CONV_TAIL = 8
MM_TM = 1088
MM_TN = 512
MOE_ROWS = 256
LN_ROWS = 272


def _bdot(a, b):
    return jnp.dot(a.astype(BF16), b.astype(BF16), preferred_element_type=F32)


def _bdot_nt(a, b):
    return lax.dot_general(a.astype(BF16), b.astype(BF16), (((1,), (1,)), ((), ())), preferred_element_type=F32)


def _bdot_tn(a, b):
    return lax.dot_general(a.astype(BF16), b.astype(BF16), (((0,), (0,)), ((), ())), preferred_element_type=F32)


def _tdot(sel, x):
    hi = x.astype(BF16)
    rest = x - hi.astype(F32)
    mid = rest.astype(BF16)
    lo = (rest - mid.astype(F32)).astype(BF16)
    d = lambda u: jnp.dot(sel, u, preferred_element_type=F32)
    return d(hi) + (d(mid) + d(lo))


def _hdot(a, b):
    return jnp.dot(a, b, precision=HIGHEST, preferred_element_type=F32)


def _iota2(shape, axis):
    return lax.broadcasted_iota(jnp.int32, shape, axis)


def _mm_kernel(x_ref, w_ref, o_ref, *, exact):
    w = w_ref[...].reshape(w_ref.shape[-2:])
    if exact:
        o_ref[...] = _hdot(x_ref[...], w)
    else:
        o_ref[...] = _bdot(x_ref[...], w)


def mm(x, w, layer=None, *, tm=MM_TM, tn=MM_TN, exact=False):
    m, k = x.shape
    n = w.shape[-1]
    tm = min(tm, m)
    tn = min(tn, n)
    if layer is None:
        w_spec = pl.BlockSpec((k, tn), lambda i, j: (0, j))
    else:
        w_spec = pl.BlockSpec((1, k, tn), lambda i, j: (layer, 0, j))
    return pl.pallas_call(
        functools.partial(_mm_kernel, exact=exact),
        grid=(pl.cdiv(m, tm), pl.cdiv(n, tn)),
        in_specs=[pl.BlockSpec((tm, k), lambda i, j: (i, 0)), w_spec],
        out_specs=pl.BlockSpec((tm, tn), lambda i, j: (i, j)),
        out_shape=jax.ShapeDtypeStruct((m, n), F32),
        compiler_params=pltpu.CompilerParams(dimension_semantics=("parallel", "parallel"),
                                             vmem_limit_bytes=VMEM_LIMIT),
        name="mm",
    )(x, w)


def _wkv7_kernel(r_ref, lw_ref, k_ref, v_ref, kk_ref, a_ref, g_ref, rk_ref, lg_ref, lb_ref, s0_ref,
                 *rest, chunk, pairs, fill_slot):
    y_ref, so_ref, s_scr = rest[-3:]
    c = pl.program_id(2)
    pair = s_scr.shape[1]
    head = pair // 2
    ps = range(pairs)
    same_head = (_iota2((pair, pair), 0) < head) == (_iota2((pair, pair), 1) < head)

    @pl.when(c == 0)
    def _():
        for p in ps:
            rows = s0_ref[0, 2 * p:2 * p + 2].reshape(pair, head)
            s_scr[p] = jnp.where(same_head, jnp.concatenate([rows, rows], axis=1), 0.0)

    row2 = _iota2((chunk, 2 * chunk), 0)
    lane2 = _iota2((chunk, 2 * chunk), 1)
    col2 = lane2 & (chunk - 1)
    strict = row2 > col2
    incl = row2 >= col2
    first = lane2 < chunk
    head0 = _iota2((chunk, pair), 1) < head
    tri = (_iota2((chunk, chunk), 0) >= _iota2((chunk, chunk), 1)).astype(BF16)
    lanes = lambda ref, p: ref[:, p * pair:(p + 1) * pair]
    rows2 = lambda t, u: jnp.concatenate([t, u], axis=0)
    stack = lambda t: rows2(jnp.where(head0, t, 0.0), jnp.where(head0, 0.0, t))
    blockdiag = lambda m: rows2(jnp.where(first, m, 0.0), jnp.where(first, 0.0, m))

    def head_sum(t):
        s0 = jnp.sum(jnp.where(head0, t, 0.0), axis=-1, keepdims=True)
        s1 = jnp.sum(jnp.where(head0, 0.0, t), axis=-1, keepdims=True)
        return jnp.where(head0, s0, s1)

    r, lw, k, v, kk, a = ([lanes(ref, p) for p in ps] for ref in (r_ref, lw_ref, k_ref, v_ref, kk_ref, a_ref))
    s = [s_scr[p] for p in ps]
    cum = [_tdot(tri, lw[p]) for p in ps]
    w_t = [jnp.exp(cum[p]) for p in ps]
    inv_w = [jnp.exp(-cum[p]) for p in ps]
    al = [-kk[p] * jnp.exp(cum[p] - lw[p]) for p in ps]
    be = [kk[p] * a[p] * inv_w[p] for p in ps]
    kh = [k[p] * inv_w[p] for p in ps]
    rt = [r[p] * w_t[p] for p in ps]
    lhs = [rows2(al[p], rt[p]) for p in ps]
    scores = [_bdot_nt(lhs[p], rows2(stack(be[p]), stack(kh[p]))) for p in ps]
    pw = [jnp.where(strict, scores[p][:chunk, :2 * chunk], 0.0) for p in ps]
    kmat = [jnp.where(strict, scores[p][:chunk, 2 * chunk:], 0.0) for p in ps]
    bmat = [jnp.where(incl, scores[p][chunk:, :2 * chunk], 0.0) for p in ps]
    vmat = [jnp.where(incl, scores[p][chunk:, 2 * chunk:], 0.0) for p in ps]
    zy = [_bdot_nt(lhs[p], s[p]) + _bdot(rows2(kmat[p], vmat[p]), stack(v[p])) for p in ps]
    m = 1
    level = lambda m: (((row2 & (2 * m - 1)) >= m) & ((col2 & (2 * m - 1)) < m)
                       & ((row2 & -(2 * m)) == (col2 & -(2 * m))))
    tinv = [jnp.where(row2 == col2, 1.0, jnp.where(level(1), pw[p], 0.0)) for p in ps]
    for _ in range(int(math.log2(chunk)) - 1):
        m *= 2
        part = [jnp.where(level(m), pw[p], 0.0) for p in ps]
        part = [_bdot(part[p], blockdiag(tinv[p])) for p in ps]
        tinv = [tinv[p] + _bdot(tinv[p], blockdiag(part[p])) for p in ps]
    u = [_bdot(tinv[p], stack(zy[p][:chunk])) for p in ps]
    y = [zy[p][chunk:] + _bdot(bmat[p], stack(u[p])) for p in ps]
    w_c = [w_t[p][chunk - 1:chunk, :] for p in ps]
    s_new = [s[p] * w_c[p] + jnp.where(same_head, _bdot_tn(rows2(u[p], v[p]), rows2(be[p] * w_c[p], kh[p] * w_c[p])), 0.0)
             for p in ps]
    mean = [head_sum(y[p]) / head for p in ps]
    dev = [y[p] - mean[p] for p in ps]
    var = [head_sum(jnp.square(dev[p])) / head for p in ps]
    bonus = [head_sum(r[p] * k[p] * lanes(rk_ref, p)) * v[p] for p in ps]
    out = [(dev[p] * lax.rsqrt(var[p] + RW_GN_EPS) * lanes(lg_ref, p) + lanes(lb_ref, p) + bonus[p]) * lanes(g_ref, p)
           for p in ps]
    for p in ps:
        y_ref[:, p * pair:(p + 1) * pair] = out[p]
        s_scr[p] = s_new[p]

    @pl.when(c == pl.num_programs(2) - 1)
    def _():
        out = _state_view(so_ref, fill_slot)
        for p in ps:
            folded = s_scr[p][:, :head] + s_scr[p][:, head:]
            out[0, 2 * p:2 * p + 2] = folded.reshape(2, head, head)


def _state_spec(block, index, layer):
    if layer is None:
        return pl.BlockSpec(block, index)
    return pl.BlockSpec((None,) + block, lambda *g: (layer,) + index(*g))


def _state_out(block, index, shape, n_inputs, stack):
    if stack is None:
        return pl.BlockSpec(block, index), jax.ShapeDtypeStruct(shape, F32), [], [], {}, None
    n_layers, slot, prev = stack
    full = jax.ShapeDtypeStruct((n_layers,) + shape, F32)
    if prev is None:
        return pl.BlockSpec((n_layers,) + block, lambda *g: (0,) + index(*g)), full, [], [], {}, slot
    spec = pl.BlockSpec((None,) + block, lambda *g: (slot,) + index(*g))
    return spec, full, [pl.BlockSpec(memory_space=pl.ANY)], [prev], {n_inputs: 1}, None


def _state_view(so_ref, fill_slot):
    if fill_slot is None:
        return so_ref
    for other in range(so_ref.shape[0]):
        if other != fill_slot:
            so_ref[other] = jnp.zeros(so_ref.shape[1:], so_ref.dtype)
    return so_ref.at[fill_slot]


def wkv7(r, lw, k, v, kk, a, g, r_k, lnx_g, lnx_b, s0, layer=None, stack=None, *, n_seq, chunk, n_chunks, pairs):
    h, n = s0.shape[-3:-1]
    width = h * n
    seq = pl.BlockSpec((chunk, 2 * pairs * n), lambda i, j, c: (i * n_chunks + c, j))
    vec = pl.BlockSpec((1, 2 * pairs * n), lambda i, j, c: (0, j))
    st_block, st_index = (1, 2 * pairs, n, n), lambda i, j, c: (i, j, 0, 0)
    args = [r, lw, k, v, kk, a, g, r_k.reshape(1, width), lnx_g.reshape(1, width), lnx_b.reshape(1, width), s0]
    so_spec, so_shape, more_specs, more_args, aliases, fill = _state_out(st_block, st_index, (n_seq, h, n, n),
                                                                         len(args), stack)
    return pl.pallas_call(
        functools.partial(_wkv7_kernel, chunk=chunk, pairs=pairs, fill_slot=fill),
        grid=(n_seq, h // (2 * pairs), n_chunks),
        in_specs=[seq] * 7 + [vec] * 3 + [_state_spec(st_block, st_index, layer)] + more_specs,
        out_specs=[seq, so_spec],
        out_shape=[jax.ShapeDtypeStruct((n_seq * n_chunks * chunk, width), F32), so_shape],
        scratch_shapes=[pltpu.VMEM((pairs, 2 * n, 2 * n), F32)],
        input_output_aliases=aliases,
        compiler_params=pltpu.CompilerParams(dimension_semantics=("parallel", "parallel", "arbitrary"),
                                             vmem_limit_bytes=VMEM_LIMIT),
        name="wkv7",
    )(*args, *more_args)


def _ssd_kernel(u_ref, prev_ref, cw_ref, cb_ref, z_ref, dt_ref, da_ref, dat_ref, e_ref, dsk_ref, nw_ref, s0_ref,
                *rest, chunk, fill_slot):
    y_ref, so_ref, s_scr, tail_scr = rest[-4:]
    c = pl.program_id(1)
    pairs, pair, n_state = s_scr.shape
    head = pair // 2
    width = pairs * pair
    n_taps = cw_ref.shape[0]
    keep = tail_scr.shape[0]
    pairs_per_group = 2 * pairs * n_state // (u_ref.shape[-1] - width)

    @pl.when(c == 0)
    def _():
        s_scr[...] = s0_ref[0].reshape(pairs, pair, n_state)
        tail_scr[...] = prev_ref[0]

    u = u_ref[...]
    ext = jnp.concatenate([tail_scr[...], u], axis=0)
    first = keep - (n_taps - 1)
    conv = cb_ref[...] + sum(cw_ref[j:j + 1, :] * ext[first + j:first + j + chunk] for j in range(n_taps))
    xbc = conv * jax.nn.sigmoid(conv)
    tail_scr[...] = u[chunk - keep:]
    row = _iota2((chunk, chunk), 0)
    col = _iota2((chunk, chunk), 1)
    incl = row >= col
    cum_all = _hdot(incl.astype(F32), da_ref[...])
    cum_t = _hdot(dat_ref[0], (row <= col).astype(F32))
    expand = e_ref[...]
    cum_lanes = _hdot(cum_all, expand)
    dt_lanes = _hdot(dt_ref[...], expand)
    x = xbc[:, :width]
    xdt = x * dt_lanes
    xdt_end = xdt * jnp.exp(cum_lanes[chunk - 1:chunk] - cum_lanes)
    ecum = jnp.exp(cum_lanes)
    bm = xbc[:, width:(width + xbc.shape[-1]) // 2]
    cm = xbc[:, (width + xbc.shape[-1]) // 2:]
    group = lambda t, p: t[:, (p // pairs_per_group) * n_state:(p // pairs_per_group + 1) * n_state]
    cb = [_bdot_nt(cm[:, g * n_state:(g + 1) * n_state], bm[:, g * n_state:(g + 1) * n_state])
          for g in range(pairs // pairs_per_group)]

    def decay_scores(h):
        seg = cum_all[:, h:h + 1] - cum_t[h:h + 1, :]
        return cb[h // (2 * pairs_per_group)] * jnp.where(incl, jnp.exp(jnp.minimum(seg, 0.0)), 0.0)

    ps = range(pairs)
    left = _iota2((chunk, pair), 1) < head
    top = _iota2((pair, n_state), 0) < head
    s = [s_scr[p] for p in ps]
    m0 = [decay_scores(2 * p) for p in ps]
    m1 = [decay_scores(2 * p + 1) for p in ps]
    xp = [xdt[:, p * pair:(p + 1) * pair] for p in ps]
    if chunk % LANES == 0:
        intra = [_bdot(jnp.concatenate([m0[p], m1[p]], axis=1),
                       jnp.concatenate([jnp.where(left, xp[p], 0.0), jnp.where(left, 0.0, xp[p])], axis=0)) for p in ps]
    else:
        intra = [_bdot(m0[p], jnp.where(left, xp[p], 0.0)) + _bdot(m1[p], jnp.where(left, 0.0, xp[p])) for p in ps]
    inter = [_bdot_nt(group(cm, p), s[p]) * ecum[:, p * pair:(p + 1) * pair] for p in ps]
    last = jnp.exp(cum_all[chunk - 1:chunk, :])
    dec = [jnp.where(top, last[:, 2 * p:2 * p + 1], last[:, 2 * p + 1:2 * p + 2]) for p in ps]
    s_new = [s[p] * dec[p] + _bdot_tn(xdt_end[:, p * pair:(p + 1) * pair], group(bm, p)) for p in ps]
    lanes = lambda t, p: t[:, p * pair:(p + 1) * pair]
    gated = [(intra[p] + inter[p] + lanes(dsk_ref, p) * lanes(x, p)) * jax.nn.silu(lanes(z_ref, p)) for p in ps]
    sq = [jnp.sum(jnp.square(gated[p]), axis=-1, keepdims=True) for p in ps]
    inv_rms = []
    for g in range(pairs // pairs_per_group):
        total = sum(sq[g * pairs_per_group:(g + 1) * pairs_per_group])
        inv_rms.append(lax.rsqrt(total / (pairs_per_group * pair) + NORM_EPS))
    for p in ps:
        y_ref[:, p * pair:(p + 1) * pair] = gated[p] * inv_rms[p // pairs_per_group] * lanes(nw_ref, p)
        s_scr[p] = s_new[p]

    @pl.when(c == pl.num_programs(1) - 1)
    def _():
        _state_view(so_ref, fill_slot)[0] = s_scr[...].reshape(2 * pairs, head, n_state)


def ssd(u, prev, conv_w, conv_b, z, dt, da, d_skip, norm_w, s0, layer=None, stack=None, *, n_seq, chunk, n_chunks):
    width = z.shape[-1]
    channels = u.shape[-1]
    n_heads = dt.shape[-1]
    head = width // n_heads
    n_state = s0.shape[-1]
    rows = n_seq * n_chunks * chunk
    dat = jnp.swapaxes(da[:rows].reshape(n_seq * n_chunks, chunk, n_heads), 1, 2)
    expand = (jnp.arange(width)[None, :] // head == jnp.arange(n_heads)[:, None]).astype(F32)
    seq = lambda w: pl.BlockSpec((chunk, w), lambda i, c: (i * n_chunks + c, 0))
    const = lambda shape: pl.BlockSpec(shape, lambda i, c: (0,) * len(shape))
    st_block, st_index = (1, n_heads, head, n_state), lambda i, c: (i, 0, 0, 0)
    args = [u, prev, conv_w, conv_b.reshape(1, channels), z, dt, da, dat, expand,
            jnp.repeat(d_skip, head).reshape(1, width), norm_w.reshape(1, width), s0]
    so_spec, so_shape, more_specs, more_args, aliases, fill = _state_out(st_block, st_index, (n_seq,) + s0.shape[-3:],
                                                                         len(args), stack)
    return pl.pallas_call(
        functools.partial(_ssd_kernel, chunk=chunk, fill_slot=fill),
        grid=(n_seq, n_chunks),
        in_specs=[seq(channels), pl.BlockSpec((1, CONV_TAIL, channels), lambda i, c: (i, 0, 0)),
                  const(conv_w.shape), const((1, channels)), seq(width), seq(n_heads), seq(n_heads),
                  pl.BlockSpec((1, n_heads, chunk), lambda i, c: (i * n_chunks + c, 0, 0)),
                  const(expand.shape), const((1, width)), const((1, width)),
                  _state_spec(st_block, st_index, layer)] + more_specs,
        out_specs=[seq(width), so_spec],
        out_shape=[jax.ShapeDtypeStruct((rows, width), F32), so_shape],
        scratch_shapes=[pltpu.VMEM((n_heads // 2, 2 * head, n_state), F32), pltpu.VMEM((CONV_TAIL, channels), F32)],
        input_output_aliases=aliases,
        compiler_params=pltpu.CompilerParams(dimension_semantics=("parallel", "arbitrary"),
                                             vmem_limit_bytes=VMEM_LIMIT),
        name="ssd",
    )(*args, *more_args)


def _gla_levels(chunk):
    halves = []
    m = chunk // 2
    while m >= 1:
        halves.append(m)
        m //= 2
    return halves


def _level_ref(cum, m):
    rows, n = cum.shape
    if m >= 8:
        c3 = cum.reshape(rows // (2 * m), 2 * m, n)
        return jnp.broadcast_to(c3[:, m - 1:m, :], c3.shape).reshape(rows, n)
    c3 = cum.reshape(rows // 8, 8, n)
    sub = lax.broadcasted_iota(jnp.int32, c3.shape, 1)
    ref = jnp.broadcast_to(c3[:, m - 1:m, :], c3.shape)
    for b0 in range(2 * m, 8, 2 * m):
        ref = jnp.where(sub >= b0, jnp.broadcast_to(c3[:, b0 + m - 1:b0 + m, :], c3.shape), ref)
    return ref.reshape(rows, n)


def _gla_kernel(q_ref, f_ref, i_ref, g_ref, lo_ref, nw_ref, s0_ref, *rest, chunk, heads, valid, fill_slot):
    y_ref, so_ref, s_scr = rest[-3:]
    c = pl.program_id(2)
    kd = s_scr.shape[1]

    @pl.when(c == 0)
    def _():
        s_scr[...] = s0_ref[0]

    row = _iota2((chunk, chunk), 0)
    col = _iota2((chunk, chunk), 1)
    trow = _iota2((chunk, 1), 0)
    tri = (row >= col).astype(BF16)
    live = trow < valid
    hs = range(heads)
    lanes = lambda ref, h: ref[:, h * kd:(h + 1) * kd]
    lower = [lanes(lo_ref, h) for h in hs]
    fg = [lower[h] + (1.0 - lower[h]) * jax.nn.sigmoid(lanes(f_ref, h)) for h in hs]
    q = [jax.nn.silu(lanes(q_ref, h)) for h in hs]
    k = [1.0 - fg[h] for h in hs]
    lf = [jnp.log(fg[h]) for h in hs]
    if valid < chunk:
        q = [jnp.where(live, q[h], 0.0) for h in hs]
        k = [jnp.where(live, k[h], 0.0) for h in hs]
        lf = [jnp.where(live, lf[h], 0.0) for h in hs]
    v = [lanes(i_ref, h) for h in hs]
    s = [s_scr[h] for h in hs]
    cum = [_tdot(tri, lf[h]) for h in hs]
    att = [jnp.zeros((chunk, chunk), F32) for _ in hs]
    for m in _gla_levels(chunk):
        second = (trow & (2 * m - 1)) >= m
        same = (row & -(2 * m)) == (col & -(2 * m))
        ref = [_level_ref(cum[h], m) for h in hs]
        ql = [jnp.where(second, q[h] * jnp.exp(jnp.where(second, jnp.minimum(cum[h] - ref[h], 0.0), 0.0)), 0.0)
              for h in hs]
        kl = [jnp.where(second, 0.0, k[h] * jnp.exp(jnp.where(second, 0.0, jnp.minimum(ref[h] - cum[h], 0.0))))
              for h in hs]
        att = [att[h] + jnp.where(same, _bdot_nt(ql[h], kl[h]), 0.0) for h in hs]
    diag = [jnp.sum(q[h] * k[h], axis=-1, keepdims=True) for h in hs]
    o = [_bdot(att[h], v[h]) + diag[h] * v[h] + _bdot(q[h] * jnp.exp(cum[h]), s[h]) for h in hs]
    tot = [cum[h][chunk - 1:chunk, :] for h in hs]
    dec = [jnp.broadcast_to(jnp.exp(tot[h]), (8, kd)).T[:, :1] for h in hs]
    s_new = [s[h] * dec[h] + _bdot_tn(k[h] * jnp.exp(tot[h] - cum[h]), v[h]) for h in hs]
    y = [o[h] * lax.rsqrt(jnp.mean(jnp.square(o[h]), -1, keepdims=True) + NORM_EPS) * lanes(nw_ref, h)
         * jax.nn.silu(lanes(g_ref, h)) for h in hs]
    for h in hs:
        y_ref[:, h * kd:(h + 1) * kd] = y[h]
        s_scr[h] = s_new[h]

    @pl.when(c == pl.num_programs(2) - 1)
    def _():
        _state_view(so_ref, fill_slot)[0] = s_scr[...]


def gla(p, lower, norm_w, s0, layer=None, stack=None, *, n_seq, chunk, n_chunks, heads, valid):
    h, kd, vd = s0.shape[-3:]
    width = h * kd
    blocks = h // heads
    seg = lambda which: pl.BlockSpec((chunk, heads * kd), lambda i, j, c: (i * n_chunks + c, which * blocks + j))
    vec = pl.BlockSpec((1, heads * kd), lambda i, j, c: (0, j))
    st_block, st_index = (1, heads, kd, vd), lambda i, j, c: (i, j, 0, 0)
    args = [p, p, p, p, lower.reshape(1, width), norm_w.reshape(1, width), s0]
    so_spec, so_shape, more_specs, more_args, aliases, fill = _state_out(st_block, st_index, (n_seq, h, kd, vd),
                                                                         len(args), stack)
    return pl.pallas_call(
        functools.partial(_gla_kernel, chunk=chunk, heads=heads, valid=valid, fill_slot=fill),
        grid=(n_seq, blocks, n_chunks),
        in_specs=[seg(0), seg(1), seg(2), seg(3), vec, vec, _state_spec(st_block, st_index, layer)] + more_specs,
        out_specs=[seg(0), so_spec],
        out_shape=[jax.ShapeDtypeStruct((n_seq * n_chunks * chunk, width), F32), so_shape],
        scratch_shapes=[pltpu.VMEM((heads, kd, vd), F32)],
        input_output_aliases=aliases,
        compiler_params=pltpu.CompilerParams(dimension_semantics=("parallel", "parallel", "arbitrary"),
                                             vmem_limit_bytes=VMEM_LIMIT),
        name="gla",
    )(*args, *more_args)


def _moe_kernel(be_ref, used_ref, xl_ref, xr_ref, wg_ref, wu_ref, wd_ref, o_ref):
    i = pl.program_id(0)

    @pl.when(used_ref[i] > 0)
    def _():
        x = jnp.concatenate([xl_ref[...], xr_ref[...]], axis=-1)
        hg = _bdot(x, wg_ref[0, 0])
        hu = _bdot(x, wu_ref[0, 0])
        hid = hg * jax.nn.sigmoid(hg) * hu
        o_ref[...] = _bdot(hid, wd_ref[0, 0])

    @pl.when(used_ref[i] == 0)
    def _():
        o_ref[...] = jnp.zeros_like(o_ref)


def moe_experts(block_e, block_used, x_left, x_right, w_gate, w_up, w_down, layer, *, rows):
    n_slots = x_left.shape[0]
    d, hid = w_gate.shape[-2:]
    grid_spec = pltpu.PrefetchScalarGridSpec(
        num_scalar_prefetch=2,
        grid=(n_slots // rows,),
        in_specs=[
            pl.BlockSpec((rows, d // 2), lambda i, be, used: (i, 0)),
            pl.BlockSpec((rows, d // 2), lambda i, be, used: (i, 0)),
            pl.BlockSpec((1, 1, d, hid), lambda i, be, used: (layer, be[i], 0, 0)),
            pl.BlockSpec((1, 1, d, hid), lambda i, be, used: (layer, be[i], 0, 0)),
            pl.BlockSpec((1, 1, hid, d), lambda i, be, used: (layer, be[i], 0, 0)),
        ],
        out_specs=pl.BlockSpec((rows, d), lambda i, be, used: (i, 0)),
    )
    return pl.pallas_call(
        _moe_kernel,
        grid_spec=grid_spec,
        out_shape=jax.ShapeDtypeStruct((n_slots, d), F32),
        compiler_params=pltpu.CompilerParams(dimension_semantics=("arbitrary",), vmem_limit_bytes=VMEM_LIMIT),
        name="moe_experts",
    )(block_e, block_used, x_left, x_right, w_gate, w_up, w_down)


def _first_max(p):
    idx = lax.broadcasted_iota(jnp.int32, p.shape, p.ndim - 1)
    top = jnp.max(p, axis=-1, keepdims=True)
    return top, jnp.min(jnp.where(p == top, idx, p.shape[-1]), axis=-1, keepdims=True)


def _prefix_counts(onehot):
    m, e = onehot.shape
    blk = LANES
    oh = onehot.reshape(m // blk, blk, e)
    r = jnp.arange(blk)
    inside = jnp.einsum('ts,bse->bte', (r[:, None] > r[None, :]).astype(F32), oh, precision=HIGHEST)
    totals = jnp.sum(oh, axis=1)
    b = jnp.arange(m // blk)
    before = jnp.einsum('ab,be->ae', (b[:, None] > b[None, :]).astype(F32), totals, precision=HIGHEST)
    return (inside + before[:, None, :]).reshape(m, e), jnp.sum(totals, axis=0)


def hier_moe(x, x_halves, wr_grp, br_grp, wr_exp, br_exp, w_gate, w_up, w_down, layer):
    t, d = x.shape
    w_route = jnp.concatenate([wr_grp, jnp.moveaxis(wr_exp, 0, 1).reshape(d, MOE_EXPERTS)], axis=1)
    logits = mm(x, w_route, exact=True)
    grp_p, grp_i = _first_max(jax.nn.softmax(logits[:, :MOE_GROUPS] + br_grp, axis=-1))
    exp_all = logits[:, MOE_GROUPS:].reshape(t, MOE_GROUPS, MOE_PER_GROUP) + br_exp
    in_grp = lax.broadcasted_iota(jnp.int32, (t, MOE_GROUPS, 1), 1) == grp_i[:, :, None]
    exp_prob = jax.nn.softmax(jnp.sum(jnp.where(in_grp, exp_all, 0.0), axis=1), axis=-1)
    p1, i1 = _first_max(exp_prob)
    taken = lax.broadcasted_iota(jnp.int32, exp_prob.shape, 1) == i1
    p2, i2 = _first_max(jnp.where(taken, -jnp.inf, exp_prob))
    top_p = jnp.concatenate([p1, p2], axis=1)
    gates = grp_p * top_p / jnp.sum(top_p, -1, keepdims=True)
    expert = grp_i * MOE_PER_GROUP + jnp.concatenate([i1, i2], axis=1)
    m = t * MOE_TOPK
    onehot = (expert.reshape(m, 1) == jnp.arange(MOE_EXPERTS)[None, :]).astype(F32)
    ranks, counts = _prefix_counts(onehot)
    padded = jnp.ceil(counts / MOE_ROWS) * MOE_ROWS
    ends = jnp.sum(jnp.where(jnp.arange(MOE_EXPERTS)[:, None] <= jnp.arange(MOE_EXPERTS)[None, :],
                             padded[:, None], 0.0), axis=0)
    slot = jnp.sum(onehot * (ranks + (ends - padded)[None, :]), axis=1).astype(jnp.int32)
    n_blocks = -(-m // MOE_ROWS) + MOE_EXPERTS
    n_slots = n_blocks * MOE_ROWS
    flat_t = lax.broadcasted_iota(jnp.int32, (t, MOE_TOPK), 0).reshape(m)
    slot_tok = (jnp.arange(n_slots, dtype=jnp.int32) % t).at[slot].set(flat_t)
    block_start = jnp.arange(n_blocks, dtype=jnp.int32) * MOE_ROWS
    ends_i = ends.astype(jnp.int32)
    block_e = jnp.minimum(jnp.sum((block_start[:, None] >= ends_i[None, :]).astype(jnp.int32), axis=1),
                          MOE_EXPERTS - 1)
    block_used = (block_start < ends_i[-1]).astype(jnp.int32)
    ys = moe_experts(block_e, block_used, x_halves[0][slot_tok], x_halves[1][slot_tok], w_gate, w_up, w_down, layer,
                     rows=MOE_ROWS)
    slot = slot.reshape(t, MOE_TOPK)
    return ys[slot[:, 0]], ys[slot[:, 1]], gates


def _ln_kernel(*refs, n_terms, gated, extra):
    x_ref = refs[0]
    terms = refs[1:1 + n_terms]
    pos = 1 + n_terms
    gate_ref = refs[pos] if gated else None
    pos += int(gated)
    g_ref, b_ref = refs[pos:pos + 2]
    outs = refs[pos + 2:]
    h = DN_ALPHA * x_ref[...]
    for i, term in enumerate(terms):
        h = h + (term[...] * gate_ref[:, i:i + 1] if gated else term[...])
    mu = jnp.mean(h, -1, keepdims=True)
    dev = h - mu
    var = jnp.mean(jnp.square(dev), -1, keepdims=True)
    y = dev * lax.rsqrt(var + NORM_EPS) * g_ref[...] + b_ref[...]
    outs[0][...] = y
    if extra == "bf16":
        outs[1][...] = y.astype(BF16)
    elif extra == "halves":
        half = y.shape[-1] // 2
        outs[1][...] = y[:, :half]
        outs[2][...] = y[:, half:]


def residual_layer_norm(x, terms, gates, g, b, *, extra):
    t, d = x.shape
    rows = pl.BlockSpec((LN_ROWS, d), lambda i: (i, 0))
    vec = pl.BlockSpec((1, d), lambda i: (0, 0))
    gate_specs = [] if gates is None else [pl.BlockSpec((LN_ROWS, gates.shape[1]), lambda i: (i, 0))]
    gate_args = [] if gates is None else [gates]
    out_shape = [jax.ShapeDtypeStruct((t, d), F32)]
    out_specs = [rows]
    if extra == "bf16":
        out_shape.append(jax.ShapeDtypeStruct((t, d), BF16))
        out_specs.append(rows)
    elif extra == "halves":
        out_shape += [jax.ShapeDtypeStruct((t, d // 2), F32)] * 2
        out_specs += [pl.BlockSpec((LN_ROWS, d // 2), lambda i: (i, 0))] * 2
    return pl.pallas_call(
        functools.partial(_ln_kernel, n_terms=len(terms), gated=gates is not None, extra=extra),
        grid=(t // LN_ROWS,),
        in_specs=[rows] * (1 + len(terms)) + gate_specs + [vec, vec],
        out_specs=out_specs,
        out_shape=out_shape,
        compiler_params=pltpu.CompilerParams(dimension_semantics=("parallel",), vmem_limit_bytes=VMEM_LIMIT),
        name="residual_layer_norm",
    )(x, *terms, *gate_args, g.reshape(1, d), b.reshape(1, d))


def _pad_len(t, length):
    return jnp.pad(t, [(0, 0), (0, length - t.shape[1])] + [(0, 0)] * (t.ndim - 2))


def _shift_rows(p, prev, b, l):
    seq = p.reshape(b, l, p.shape[-1])
    return jnp.concatenate([prev[:, None, :], seq[:, :-1]], axis=1).reshape(p.shape), seq[:, -1]


def even_mixer(x, lens, e, shift0, wkv0, conv0, ssm0, v_first, vres, w_in, w_out, mu, w0, w_up, a0, a_up, g_up,
               k_k, k_a, r_k, lnx_g, lnx_b, conv_w, conv_b, dt_bias, a_log, d_skip, norm_w):
    (bp, lp, _), (bs, ls, _) = lens
    np_rows = bp * lp
    t = x.shape[0]
    p = mm(x, w_in, e)
    p_rw, p_ssm = p[:, :RW_PROJ], p[:, RW_PROJ:]
    prev_p, shift_p = _shift_rows(p_rw[:np_rows], shift0[0], bp, lp)
    prev_s, shift_s = _shift_rows(p_rw[np_rows:], shift0[1], bs, ls)
    pm = p_rw + (jnp.concatenate([prev_p, prev_s], axis=0) - p_rw) * mu
    cuts = [RW_WIDTH, 2 * RW_WIDTH, 3 * RW_WIDTH, 3 * RW_WIDTH + RW_LORA_W, 3 * RW_WIDTH + RW_LORA_W + RW_LORA_A]
    r, k, v, wd, ad, gd = jnp.split(pm, cuts, axis=-1)
    log_w = -jax.nn.softplus(-(w0 + mm(jnp.tanh(wd), w_up))) - 0.5
    lw = -jnp.exp(log_w)
    if vres is None:
        v_first = v
    else:
        v0, v_down, v_up = vres
        v = v + (v_first - v) * jax.nn.sigmoid(v0 + mm(mm(v, v_down), v_up))
    a = jax.nn.sigmoid(a0 + mm(ad, a_up))
    g = mm(jax.nn.sigmoid(gd), g_up)
    heads = lambda u: u.reshape(t, RW_HEADS, RW_HEAD)
    kk = heads(k * k_k)
    kk = kk * lax.rsqrt(jnp.maximum(jnp.sum(jnp.square(kk), -1, keepdims=True), 1e-24))
    k = k * (1.0 + (a - 1.0) * k_a)
    seqs = [r, lw, k, v, kk.reshape(t, RW_WIDTH), a, g]
    chunk_p = min(WKV_CHUNK, lp)
    o_p, wkv_p = wkv7(*seqs, r_k, lnx_g, lnx_b, *wkv0[0], n_seq=bp, chunk=chunk_p, n_chunks=lp // chunk_p,
                      pairs=RW_HEADS // 2)
    padded = [_pad_len(u[np_rows:].reshape(bs, ls, RW_WIDTH), SAMPLE_PAD).reshape(bs * SAMPLE_PAD, RW_WIDTH)
              for u in seqs]
    o_s, wkv_s = wkv7(*padded, r_k, lnx_g, lnx_b, *wkv0[1], n_seq=bs, chunk=SAMPLE_PAD, n_chunks=1,
                      pairs=RW_HEADS // 2)
    o_rw = jnp.concatenate([o_p, o_s.reshape(bs, SAMPLE_PAD, RW_WIDTH)[:, :ls].reshape(bs * ls, RW_WIDTH)], axis=0)
    z = p_ssm[:, :SSM_WIDTH]
    xbc_raw = p_ssm[:, SSM_WIDTH:SSM_WIDTH + SSM_CONV_CH]
    dt_raw = p_ssm[:, SSM_WIDTH + SSM_CONV_CH:]

    def conv_io(u, buf, b, l):
        full = jnp.concatenate([buf, u.reshape(b, l, SSM_CONV_CH)[:, max(l - (SSM_CONV - 1), 0):]], axis=1)
        return jnp.pad(buf, [(0, 0), (CONV_TAIL - (SSM_CONV - 1), 0), (0, 0)]), full[:, full.shape[1] - (SSM_CONV - 1):]

    prev_p, conv_p = conv_io(xbc_raw[:np_rows], conv0[0], bp, lp)
    prev_s, conv_s = conv_io(xbc_raw[np_rows:], conv0[1], bs, ls)
    dt = jax.nn.softplus(dt_raw + dt_bias)
    seqs = [xbc_raw, z, dt, dt * -jnp.exp(a_log)]
    chunk_p = min(SSD_CHUNK, lp)
    y_p, ssm_p = ssd(seqs[0], prev_p, conv_w, conv_b, *seqs[1:], d_skip, norm_w, *ssm0[0], n_seq=bp, chunk=chunk_p,
                     n_chunks=lp // chunk_p)
    padded = [_pad_len(u[np_rows:].reshape(bs, ls, u.shape[-1]), SAMPLE_PAD).reshape(bs * SAMPLE_PAD, u.shape[-1])
              for u in seqs]
    y_s, ssm_s = ssd(padded[0], prev_s, conv_w, conv_b, *padded[1:], d_skip, norm_w, *ssm0[1], n_seq=bs,
                     chunk=SAMPLE_PAD, n_chunks=1)
    y = jnp.concatenate([y_p, y_s.reshape(bs, SAMPLE_PAD, SSM_WIDTH)[:, :ls].reshape(bs * ls, SSM_WIDTH)], axis=0)
    mix = mm(jnp.concatenate([o_rw, y], axis=-1).astype(BF16), w_out, e, tm=MM_TM // 2)
    return mix, v_first, (shift_p, shift_s), (wkv_p, wkv_s), (conv_p, conv_s), (ssm_p, ssm_s)


def hgrn2_mixer(x, lens, o, s0, w_in, w_out, lower, norm_w):
    (bp, lp, _), (bs, ls, _) = lens
    p = mm(x, w_in, o)
    y_p, hg_p = gla(p, lower, norm_w, *s0[0], n_seq=bp, chunk=GLA_CHUNK, n_chunks=lp // GLA_CHUNK,
                    heads=GLA_HEADS_PER_STEP, valid=GLA_CHUNK)
    p_s = _pad_len(p[bp * lp:].reshape(bs, ls, p.shape[-1]), SAMPLE_PAD).reshape(bs * SAMPLE_PAD, p.shape[-1])
    y_s, hg_s = gla(p_s, lower, norm_w, *s0[1], n_seq=bs, chunk=SAMPLE_PAD, n_chunks=1, heads=HG_HEADS, valid=ls)
    y_s = y_s.reshape(bs, SAMPLE_PAD, -1)[:, :ls].reshape(bs * ls, -1)
    return mm(jnp.concatenate([y_p, y_s], axis=0).astype(BF16), w_out, o), (hg_p, hg_s)


def kernel(x_prompt, x_sample, state_rwkv_shift, state_rwkv_wkv, state_ssm_conv, state_ssm, state_hgrn,
           ev_w_in, ev_w_out, rw_mu, rw_w0, rw_w_up, rw_a0, rw_a_up, rw_g_up, rw_k_k, rw_k_a, rw_r_k,
           rw_lnx_g, rw_lnx_b, rw_v0, rw_v_down, rw_v_up, ssm_conv_w, ssm_conv_b, ssm_dt_bias, ssm_a_log,
           ssm_d, ssm_norm_w, od_w_in, od_w_out, hg_lower_bounds, hg_norm_w, ln1_g, ln1_b, ln2_g, ln2_b,
           moe_wr_grp, moe_br_grp, moe_wr_exp, moe_br_exp, moe_w_gate, moe_w_up, moe_w_down):
    lb_soft = jax.nn.softmax(hg_lower_bounds, axis=0)
    lower = jnp.cumsum(lb_soft, axis=0) - lb_soft[0]
    bp, lp, d = x_prompt.shape
    bs, ls, _ = x_sample.shape
    lens = ((bp, lp, None), (bs, ls, None))
    x = jnp.concatenate([x_prompt.reshape(bp * lp, d), x_sample.reshape(bs * ls, d)], axis=0)
    zeros_p = lambda s: jnp.zeros((bp,) + s.shape[2:], s.dtype)
    x_bf16 = x.astype(BF16)
    v_first = None
    shifts, wkvs, convs, ssms, hgs = [], [], [], [], []
    wkv_all = ssm_all = hg_all = None
    for layer in range(DEPTH):
        if layer % 2 == 0:
            e = layer // 2
            n_even = state_rwkv_wkv.shape[0]
            vres = None if e == 0 else (rw_v0[e - 1], rw_v_down[e - 1], rw_v_up[e - 1])
            mix, v_first, s_sh, s_wkv, s_cv, s_ss = even_mixer(
                x_bf16, lens, e, (zeros_p(state_rwkv_shift), state_rwkv_shift[e]),
                ((zeros_p(state_rwkv_wkv), None), (state_rwkv_wkv, e, (n_even, e, wkv_all))),
                (zeros_p(state_ssm_conv), state_ssm_conv[e]),
                ((zeros_p(state_ssm), None), (state_ssm, e, (n_even, e, ssm_all))), v_first, vres,
                ev_w_in, ev_w_out, rw_mu[e], rw_w0[e], rw_w_up[e], rw_a0[e], rw_a_up[e], rw_g_up[e], rw_k_k[e],
                rw_k_a[e], rw_r_k[e], rw_lnx_g[e], rw_lnx_b[e], ssm_conv_w[e], ssm_conv_b[e], ssm_dt_bias[e],
                ssm_a_log[e], ssm_d[e], ssm_norm_w[e])
            shifts.append(s_sh)
            convs.append(s_cv)
            wkvs.append(s_wkv[0])
            ssms.append(s_ss[0])
            wkv_all, ssm_all = s_wkv[1], s_ss[1]
        else:
            o = layer // 2
            mix, s_hg = hgrn2_mixer(x_bf16, lens, o,
                                    ((zeros_p(state_hgrn), None), (state_hgrn, o, (state_hgrn.shape[0], o, hg_all))),
                                    od_w_in, od_w_out, lower[layer], hg_norm_w[o])
            hgs.append(s_hg[0])
            hg_all = s_hg[1]
        x, x_lo, x_hi = residual_layer_norm(x, [mix], None, ln1_g[layer], ln1_b[layer], extra="halves")
        y0, y1, gates = hier_moe(x, (x_lo, x_hi), moe_wr_grp[layer], moe_br_grp[layer], moe_wr_exp[layer], moe_br_exp[layer],
                                 moe_w_gate, moe_w_up, moe_w_down, layer)
        x, x_bf16 = residual_layer_norm(x, [y0, y1], gates, ln2_g[layer], ln2_b[layer], extra="bf16")
    stack = lambda pairs, j: jnp.stack([p[j] for p in pairs])
    np_rows = bp * lp
    return (x[:np_rows].reshape(bp, lp, d), x[np_rows:].reshape(bs, ls, d),
            stack(shifts, 0), jnp.stack(wkvs), stack(convs, 0), jnp.stack(ssms), jnp.stack(hgs),
            stack(shifts, 1), wkv_all, stack(convs, 1), ssm_all, hg_all)
```

```python
import functools
import math

import jax
import jax.numpy as jnp
from jax import lax
from jax.experimental import pallas as pl
from jax.experimental.pallas import tpu as pltpu

F32 = jnp.float32
BF16 = jnp.bfloat16
HIGHEST = lax.Precision.HIGHEST

D_MODEL = 2048
DEPTH = 4
RW_HEAD = 64
RW_WIDTH = D_MODEL // 2
RW_HEADS = RW_WIDTH // RW_HEAD
RW_LORA_W = 64
RW_LORA_A = 64
RW_LORA_G = 160
RW_PROJ = 3 * RW_WIDTH + RW_LORA_W + RW_LORA_A + RW_LORA_G
RW_GN_EPS = 64e-5
SSM_WIDTH = D_MODEL
SSM_HEAD = 64
SSM_HEADS = SSM_WIDTH // SSM_HEAD
SSM_STATE = 128
SSM_GROUPS = 4
SSM_CONV = 4
SSM_CONV_CH = SSM_WIDTH + 2 * SSM_GROUPS * SSM_STATE
HG_WIDTH = D_MODEL
HG_EXPAND = 128
HG_HEADS = HG_WIDTH // HG_EXPAND
MOE_GROUPS = 4
MOE_PER_GROUP = 8
MOE_EXPERTS = MOE_GROUPS * MOE_PER_GROUP
MOE_TOPK = 2
MOE_HIDDEN = D_MODEL // 4
DN_ALPHA = (2 * DEPTH) ** 0.25
NORM_EPS = 1e-5

LANES = 128
VMEM_LIMIT = 52 * 1024 * 1024

WKV_CHUNK = 64
SSD_CHUNK = 128
GLA_CHUNK = 64
GLA_HEADS_PER_STEP = 16
SAMPLE_PAD = 8
CONV_TAIL = 8
MM_TM = 1088
MM_TN = 512
MOE_ROWS = 256
LN_ROWS = 272


def _bdot(a, b):
    return jnp.dot(a.astype(BF16), b.astype(BF16), preferred_element_type=F32)


def _bdot_nt(a, b):
    return lax.dot_general(a.astype(BF16), b.astype(BF16), (((1,), (1,)), ((), ())), preferred_element_type=F32)


def _bdot_tn(a, b):
    return lax.dot_general(a.astype(BF16), b.astype(BF16), (((0,), (0,)), ((), ())), preferred_element_type=F32)


def _tdot(sel, x):
    hi = x.astype(BF16)
    rest = x - hi.astype(F32)
    mid = rest.astype(BF16)
    lo = (rest - mid.astype(F32)).astype(BF16)
    d = lambda u: jnp.dot(sel, u, preferred_element_type=F32)
    return d(hi) + (d(mid) + d(lo))


def _hdot(a, b):
    return jnp.dot(a, b, precision=HIGHEST, preferred_element_type=F32)


def _iota2(shape, axis):
    return lax.broadcasted_iota(jnp.int32, shape, axis)


def _mm_kernel(x_ref, w_ref, o_ref):
    o_ref[...] = _bdot(x_ref[...], w_ref[...].reshape(w_ref.shape[-2:]))


def mm(x, w, layer=None, *, tm=MM_TM, tn=MM_TN):
    m, k = x.shape
    n = w.shape[-1]
    tm = min(tm, m)
    tn = min(tn, n)
    if layer is None:
        w_spec = pl.BlockSpec((k, tn), lambda i, j: (0, j))
    else:
        w_spec = pl.BlockSpec((1, k, tn), lambda i, j: (layer, 0, j))
    return pl.pallas_call(
        _mm_kernel,
        grid=(pl.cdiv(m, tm), pl.cdiv(n, tn)),
        in_specs=[pl.BlockSpec((tm, k), lambda i, j: (i, 0)), w_spec],
        out_specs=pl.BlockSpec((tm, tn), lambda i, j: (i, j)),
        out_shape=jax.ShapeDtypeStruct((m, n), F32),
        compiler_params=pltpu.CompilerParams(dimension_semantics=("parallel", "parallel"),
                                             vmem_limit_bytes=VMEM_LIMIT),
        name="mm",
    )(x, w)


def _wkv7_kernel(r_ref, lw_ref, k_ref, v_ref, kk_ref, a_ref, g_ref, rk_ref, lg_ref, lb_ref, s0_ref,
                 *rest, chunk, pairs, fill_slot):
    y_ref, so_ref, s_scr = rest[-3:]
    c = pl.program_id(2)
    pair = s_scr.shape[1]
    head = pair // 2
    ps = range(pairs)
    same_head = (_iota2((pair, pair), 0) < head) == (_iota2((pair, pair), 1) < head)

    @pl.when(c == 0)
    def _():
        for p in ps:
            rows = s0_ref[0, 2 * p:2 * p + 2].reshape(pair, head)
            s_scr[p] = jnp.where(same_head, jnp.concatenate([rows, rows], axis=1), 0.0)

    row2 = _iota2((chunk, 2 * chunk), 0)
    lane2 = _iota2((chunk, 2 * chunk), 1)
    col2 = lane2 & (chunk - 1)
    strict = row2 > col2
    incl = row2 >= col2
    first = lane2 < chunk
    head0 = _iota2((chunk, pair), 1) < head
    tri = (_iota2((chunk, chunk), 0) >= _iota2((chunk, chunk), 1)).astype(BF16)
    lanes = lambda ref, p: ref[:, p * pair:(p + 1) * pair]
    rows2 = lambda t, u: jnp.concatenate([t, u], axis=0)
    stack = lambda t: rows2(jnp.where(head0, t, 0.0), jnp.where(head0, 0.0, t))
    blockdiag = lambda m: rows2(jnp.where(first, m, 0.0), jnp.where(first, 0.0, m))

    def head_sum(t):
        s0 = jnp.sum(jnp.where(head0, t, 0.0), axis=-1, keepdims=True)
        s1 = jnp.sum(jnp.where(head0, 0.0, t), axis=-1, keepdims=True)
        return jnp.where(head0, s0, s1)

    r, lw, k, v, kk, a = ([lanes(ref, p) for p in ps] for ref in (r_ref, lw_ref, k_ref, v_ref, kk_ref, a_ref))
    s = [s_scr[p] for p in ps]
    cum = [_tdot(tri, lw[p]) for p in ps]
    w_t = [jnp.exp(cum[p]) for p in ps]
    inv_w = [jnp.exp(-cum[p]) for p in ps]
    al = [-kk[p] * jnp.exp(cum[p] - lw[p]) for p in ps]
    be = [kk[p] * a[p] * inv_w[p] for p in ps]
    kh = [k[p] * inv_w[p] for p in ps]
    rt = [r[p] * w_t[p] for p in ps]
    lhs = [rows2(al[p], rt[p]) for p in ps]
    scores = [_bdot_nt(lhs[p], rows2(stack(be[p]), stack(kh[p]))) for p in ps]
    pw = [jnp.where(strict, scores[p][:chunk, :2 * chunk], 0.0) for p in ps]
    kmat = [jnp.where(strict, scores[p][:chunk, 2 * chunk:], 0.0) for p in ps]
    bmat = [jnp.where(incl, scores[p][chunk:, :2 * chunk], 0.0) for p in ps]
    vmat = [jnp.where(incl, scores[p][chunk:, 2 * chunk:], 0.0) for p in ps]
    zy = [_bdot_nt(lhs[p], s[p]) + _bdot(rows2(kmat[p], vmat[p]), stack(v[p])) for p in ps]
    m = 1
    level = lambda m: (((row2 & (2 * m - 1)) >= m) & ((col2 & (2 * m - 1)) < m)
                       & ((row2 & -(2 * m)) == (col2 & -(2 * m))))
    tinv = [jnp.where(row2 == col2, 1.0, jnp.where(level(1), pw[p], 0.0)) for p in ps]
    for _ in range(int(math.log2(chunk)) - 1):
        m *= 2
        part = [jnp.where(level(m), pw[p], 0.0) for p in ps]
        part = [_bdot(part[p], blockdiag(tinv[p])) for p in ps]
        tinv = [tinv[p] + _bdot(tinv[p], blockdiag(part[p])) for p in ps]
    u = [_bdot(tinv[p], stack(zy[p][:chunk])) for p in ps]
    y = [zy[p][chunk:] + _bdot(bmat[p], stack(u[p])) for p in ps]
    w_c = [w_t[p][chunk - 1:chunk, :] for p in ps]
    s_new = [s[p] * w_c[p] + jnp.where(same_head, _bdot_tn(rows2(u[p], v[p]), rows2(be[p] * w_c[p], kh[p] * w_c[p])), 0.0)
             for p in ps]
    mean = [head_sum(y[p]) / head for p in ps]
    dev = [y[p] - mean[p] for p in ps]
    var = [head_sum(jnp.square(dev[p])) / head for p in ps]
    bonus = [head_sum(r[p] * k[p] * lanes(rk_ref, p)) * v[p] for p in ps]
    out = [(dev[p] * lax.rsqrt(var[p] + RW_GN_EPS) * lanes(lg_ref, p) + lanes(lb_ref, p) + bonus[p]) * lanes(g_ref, p)
           for p in ps]
    for p in ps:
        y_ref[:, p * pair:(p + 1) * pair] = out[p]
        s_scr[p] = s_new[p]

    @pl.when(c == pl.num_programs(2) - 1)
    def _():
        out = _state_view(so_ref, fill_slot)
        for p in ps:
            folded = s_scr[p][:, :head] + s_scr[p][:, head:]
            out[0, 2 * p:2 * p + 2] = folded.reshape(2, head, head)


def _state_spec(block, index, layer):
    if layer is None:
        return pl.BlockSpec(block, index)
    return pl.BlockSpec((None,) + block, lambda *g: (layer,) + index(*g))


def _state_out(block, index, shape, n_inputs, stack):
    if stack is None:
        return pl.BlockSpec(block, index), jax.ShapeDtypeStruct(shape, F32), [], [], {}, None
    n_layers, slot, prev = stack
    full = jax.ShapeDtypeStruct((n_layers,) + shape, F32)
    if prev is None:
        return pl.BlockSpec((n_layers,) + block, lambda *g: (0,) + index(*g)), full, [], [], {}, slot
    spec = pl.BlockSpec((None,) + block, lambda *g: (slot,) + index(*g))
    return spec, full, [pl.BlockSpec(memory_space=pl.ANY)], [prev], {n_inputs: 1}, None


def _state_view(so_ref, fill_slot):
    if fill_slot is None:
        return so_ref
    for other in range(so_ref.shape[0]):
        if other != fill_slot:
            so_ref[other] = jnp.zeros(so_ref.shape[1:], so_ref.dtype)
    return so_ref.at[fill_slot]


def wkv7(r, lw, k, v, kk, a, g, r_k, lnx_g, lnx_b, s0, layer=None, stack=None, *, n_seq, chunk, n_chunks, pairs):
    h, n = s0.shape[-3:-1]
    width = h * n
    seq = pl.BlockSpec((chunk, 2 * pairs * n), lambda i, j, c: (i * n_chunks + c, j))
    vec = pl.BlockSpec((1, 2 * pairs * n), lambda i, j, c: (0, j))
    st_block, st_index = (1, 2 * pairs, n, n), lambda i, j, c: (i, j, 0, 0)
    args = [r, lw, k, v, kk, a, g, r_k.reshape(1, width), lnx_g.reshape(1, width), lnx_b.reshape(1, width), s0]
    so_spec, so_shape, more_specs, more_args, aliases, fill = _state_out(st_block, st_index, (n_seq, h, n, n),
                                                                         len(args), stack)
    return pl.pallas_call(
        functools.partial(_wkv7_kernel, chunk=chunk, pairs=pairs, fill_slot=fill),
        grid=(n_seq, h // (2 * pairs), n_chunks),
        in_specs=[seq] * 7 + [vec] * 3 + [_state_spec(st_block, st_index, layer)] + more_specs,
        out_specs=[seq, so_spec],
        out_shape=[jax.ShapeDtypeStruct((n_seq * n_chunks * chunk, width), F32), so_shape],
        scratch_shapes=[pltpu.VMEM((pairs, 2 * n, 2 * n), F32)],
        input_output_aliases=aliases,
        compiler_params=pltpu.CompilerParams(dimension_semantics=("parallel", "parallel", "arbitrary"),
                                             vmem_limit_bytes=VMEM_LIMIT),
        name="wkv7",
    )(*args, *more_args)


def _ssd_kernel(u_ref, prev_ref, cw_ref, cb_ref, z_ref, dt_ref, da_ref, dat_ref, e_ref, dsk_ref, nw_ref, s0_ref,
                *rest, chunk, fill_slot):
    y_ref, so_ref, s_scr, tail_scr = rest[-4:]
    c = pl.program_id(1)
    pairs, pair, n_state = s_scr.shape
    head = pair // 2
    width = pairs * pair
    n_taps = cw_ref.shape[0]
    keep = tail_scr.shape[0]
    pairs_per_group = 2 * pairs * n_state // (u_ref.shape[-1] - width)

    @pl.when(c == 0)
    def _():
        s_scr[...] = s0_ref[0].reshape(pairs, pair, n_state)
        tail_scr[...] = prev_ref[0]

    u = u_ref[...]
    ext = jnp.concatenate([tail_scr[...], u], axis=0)
    first = keep - (n_taps - 1)
    conv = cb_ref[...] + sum(cw_ref[j:j + 1, :] * ext[first + j:first + j + chunk] for j in range(n_taps))
    xbc = conv * jax.nn.sigmoid(conv)
    tail_scr[...] = u[chunk - keep:]
    row = _iota2((chunk, chunk), 0)
    col = _iota2((chunk, chunk), 1)
    incl = row >= col
    cum_all = _hdot(incl.astype(F32), da_ref[...])
    cum_t = _hdot(dat_ref[0], (row <= col).astype(F32))
    expand = e_ref[...]
    cum_lanes = _hdot(cum_all, expand)
    dt_lanes = _hdot(dt_ref[...], expand)
    x = xbc[:, :width]
    xdt = x * dt_lanes
    xdt_end = xdt * jnp.exp(cum_lanes[chunk - 1:chunk] - cum_lanes)
    ecum = jnp.exp(cum_lanes)
    bm = xbc[:, width:(width + xbc.shape[-1]) // 2]
    cm = xbc[:, (width + xbc.shape[-1]) // 2:]
    group = lambda t, p: t[:, (p // pairs_per_group) * n_state:(p // pairs_per_group + 1) * n_state]
    cb = [_bdot_nt(cm[:, g * n_state:(g + 1) * n_state], bm[:, g * n_state:(g + 1) * n_state])
          for g in range(pairs // pairs_per_group)]

    def decay_scores(h):
        seg = cum_all[:, h:h + 1] - cum_t[h:h + 1, :]
        return cb[h // (2 * pairs_per_group)] * jnp.where(incl, jnp.exp(jnp.minimum(seg, 0.0)), 0.0)

    ps = range(pairs)
    left = _iota2((chunk, pair), 1) < head
    top = _iota2((pair, n_state), 0) < head
    s = [s_scr[p] for p in ps]
    m0 = [decay_scores(2 * p) for p in ps]
    m1 = [decay_scores(2 * p + 1) for p in ps]
    xp = [xdt[:, p * pair:(p + 1) * pair] for p in ps]
    if chunk % LANES == 0:
        intra = [_bdot(jnp.concatenate([m0[p], m1[p]], axis=1),
                       jnp.concatenate([jnp.where(left, xp[p], 0.0), jnp.where(left, 0.0, xp[p])], axis=0)) for p in ps]
    else:
        intra = [_bdot(m0[p], jnp.where(left, xp[p], 0.0)) + _bdot(m1[p], jnp.where(left, 0.0, xp[p])) for p in ps]
    inter = [_bdot_nt(group(cm, p), s[p]) * ecum[:, p * pair:(p + 1) * pair] for p in ps]
    last = jnp.exp(cum_all[chunk - 1:chunk, :])
    dec = [jnp.where(top, last[:, 2 * p:2 * p + 1], last[:, 2 * p + 1:2 * p + 2]) for p in ps]
    s_new = [s[p] * dec[p] + _bdot_tn(xdt_end[:, p * pair:(p + 1) * pair], group(bm, p)) for p in ps]
    lanes = lambda t, p: t[:, p * pair:(p + 1) * pair]
    gated = [(intra[p] + inter[p] + lanes(dsk_ref, p) * lanes(x, p)) * jax.nn.silu(lanes(z_ref, p)) for p in ps]
    sq = [jnp.sum(jnp.square(gated[p]), axis=-1, keepdims=True) for p in ps]
    inv_rms = []
    for g in range(pairs // pairs_per_group):
        total = sum(sq[g * pairs_per_group:(g + 1) * pairs_per_group])
        inv_rms.append(lax.rsqrt(total / (pairs_per_group * pair) + NORM_EPS))
    for p in ps:
        y_ref[:, p * pair:(p + 1) * pair] = gated[p] * inv_rms[p // pairs_per_group] * lanes(nw_ref, p)
        s_scr[p] = s_new[p]

    @pl.when(c == pl.num_programs(1) - 1)
    def _():
        _state_view(so_ref, fill_slot)[0] = s_scr[...].reshape(2 * pairs, head, n_state)


def ssd(u, prev, conv_w, conv_b, z, dt, da, d_skip, norm_w, s0, layer=None, stack=None, *, n_seq, chunk, n_chunks):
    width = z.shape[-1]
    channels = u.shape[-1]
    n_heads = dt.shape[-1]
    head = width // n_heads
    n_state = s0.shape[-1]
    rows = n_seq * n_chunks * chunk
    dat = jnp.swapaxes(da[:rows].reshape(n_seq * n_chunks, chunk, n_heads), 1, 2)
    expand = (jnp.arange(width)[None, :] // head == jnp.arange(n_heads)[:, None]).astype(F32)
    seq = lambda w: pl.BlockSpec((chunk, w), lambda i, c: (i * n_chunks + c, 0))
    const = lambda shape: pl.BlockSpec(shape, lambda i, c: (0,) * len(shape))
    st_block, st_index = (1, n_heads, head, n_state), lambda i, c: (i, 0, 0, 0)
    args = [u, prev, conv_w, conv_b.reshape(1, channels), z, dt, da, dat, expand,
            jnp.repeat(d_skip, head).reshape(1, width), norm_w.reshape(1, width), s0]
    so_spec, so_shape, more_specs, more_args, aliases, fill = _state_out(st_block, st_index, (n_seq,) + s0.shape[-3:],
                                                                         len(args), stack)
    return pl.pallas_call(
        functools.partial(_ssd_kernel, chunk=chunk, fill_slot=fill),
        grid=(n_seq, n_chunks),
        in_specs=[seq(channels), pl.BlockSpec((1, CONV_TAIL, channels), lambda i, c: (i, 0, 0)),
                  const(conv_w.shape), const((1, channels)), seq(width), seq(n_heads), seq(n_heads),
                  pl.BlockSpec((1, n_heads, chunk), lambda i, c: (i * n_chunks + c, 0, 0)),
                  const(expand.shape), const((1, width)), const((1, width)),
                  _state_spec(st_block, st_index, layer)] + more_specs,
        out_specs=[seq(width), so_spec],
        out_shape=[jax.ShapeDtypeStruct((rows, width), F32), so_shape],
        scratch_shapes=[pltpu.VMEM((n_heads // 2, 2 * head, n_state), F32), pltpu.VMEM((CONV_TAIL, channels), F32)],
        input_output_aliases=aliases,
        compiler_params=pltpu.CompilerParams(dimension_semantics=("parallel", "arbitrary"),
                                             vmem_limit_bytes=VMEM_LIMIT),
        name="ssd",
    )(*args, *more_args)


def _gla_levels(chunk):
    halves = []
    m = chunk // 2
    while m >= 1:
        halves.append(m)
        m //= 2
    return halves


def _level_ref(cum, m):
    rows, n = cum.shape
    if m >= 8:
        c3 = cum.reshape(rows // (2 * m), 2 * m, n)
        return jnp.broadcast_to(c3[:, m - 1:m, :], c3.shape).reshape(rows, n)
    c3 = cum.reshape(rows // 8, 8, n)
    sub = lax.broadcasted_iota(jnp.int32, c3.shape, 1)
    ref = jnp.broadcast_to(c3[:, m - 1:m, :], c3.shape)
    for b0 in range(2 * m, 8, 2 * m):
        ref = jnp.where(sub >= b0, jnp.broadcast_to(c3[:, b0 + m - 1:b0 + m, :], c3.shape), ref)
    return ref.reshape(rows, n)


def _gla_kernel(q_ref, f_ref, i_ref, g_ref, lo_ref, nw_ref, s0_ref, *rest, chunk, heads, valid, fill_slot):
    y_ref, so_ref, s_scr = rest[-3:]
    c = pl.program_id(2)
    kd = s_scr.shape[1]

    @pl.when(c == 0)
    def _():
        s_scr[...] = s0_ref[0]

    row = _iota2((chunk, chunk), 0)
    col = _iota2((chunk, chunk), 1)
    trow = _iota2((chunk, 1), 0)
    tri = (row >= col).astype(BF16)
    live = trow < valid
    hs = range(heads)
    lanes = lambda ref, h: ref[:, h * kd:(h + 1) * kd]
    lower = [lanes(lo_ref, h) for h in hs]
    fg = [lower[h] + (1.0 - lower[h]) * jax.nn.sigmoid(lanes(f_ref, h)) for h in hs]
    q = [jax.nn.silu(lanes(q_ref, h)) for h in hs]
    k = [1.0 - fg[h] for h in hs]
    lf = [jnp.log(fg[h]) for h in hs]
    if valid < chunk:
        q = [jnp.where(live, q[h], 0.0) for h in hs]
        k = [jnp.where(live, k[h], 0.0) for h in hs]
        lf = [jnp.where(live, lf[h], 0.0) for h in hs]
    v = [lanes(i_ref, h) for h in hs]
    s = [s_scr[h] for h in hs]
    cum = [_tdot(tri, lf[h]) for h in hs]
    att = [jnp.zeros((chunk, chunk), F32) for _ in hs]
    for m in _gla_levels(chunk):
        second = (trow & (2 * m - 1)) >= m
        same = (row & -(2 * m)) == (col & -(2 * m))
        ref = [_level_ref(cum[h], m) for h in hs]
        ql = [jnp.where(second, q[h] * jnp.exp(jnp.where(second, jnp.minimum(cum[h] - ref[h], 0.0), 0.0)), 0.0)
              for h in hs]
        kl = [jnp.where(second, 0.0, k[h] * jnp.exp(jnp.where(second, 0.0, jnp.minimum(ref[h] - cum[h], 0.0))))
              for h in hs]
        att = [att[h] + jnp.where(same, _bdot_nt(ql[h], kl[h]), 0.0) for h in hs]
    diag = [jnp.sum(q[h] * k[h], axis=-1, keepdims=True) for h in hs]
    o = [_bdot(att[h], v[h]) + diag[h] * v[h] + _bdot(q[h] * jnp.exp(cum[h]), s[h]) for h in hs]
    tot = [cum[h][chunk - 1:chunk, :] for h in hs]
    dec = [jnp.broadcast_to(jnp.exp(tot[h]), (8, kd)).T[:, :1] for h in hs]
    s_new = [s[h] * dec[h] + _bdot_tn(k[h] * jnp.exp(tot[h] - cum[h]), v[h]) for h in hs]
    y = [o[h] * lax.rsqrt(jnp.mean(jnp.square(o[h]), -1, keepdims=True) + NORM_EPS) * lanes(nw_ref, h)
         * jax.nn.silu(lanes(g_ref, h)) for h in hs]
    for h in hs:
        y_ref[:, h * kd:(h + 1) * kd] = y[h]
        s_scr[h] = s_new[h]

    @pl.when(c == pl.num_programs(2) - 1)
    def _():
        _state_view(so_ref, fill_slot)[0] = s_scr[...]


def gla(p, lower, norm_w, s0, layer=None, stack=None, *, n_seq, chunk, n_chunks, heads, valid):
    h, kd, vd = s0.shape[-3:]
    width = h * kd
    blocks = h // heads
    seg = lambda which: pl.BlockSpec((chunk, heads * kd), lambda i, j, c: (i * n_chunks + c, which * blocks + j))
    vec = pl.BlockSpec((1, heads * kd), lambda i, j, c: (0, j))
    st_block, st_index = (1, heads, kd, vd), lambda i, j, c: (i, j, 0, 0)
    args = [p, p, p, p, lower.reshape(1, width), norm_w.reshape(1, width), s0]
    so_spec, so_shape, more_specs, more_args, aliases, fill = _state_out(st_block, st_index, (n_seq, h, kd, vd),
                                                                         len(args), stack)
    return pl.pallas_call(
        functools.partial(_gla_kernel, chunk=chunk, heads=heads, valid=valid, fill_slot=fill),
        grid=(n_seq, blocks, n_chunks),
        in_specs=[seg(0), seg(1), seg(2), seg(3), vec, vec, _state_spec(st_block, st_index, layer)] + more_specs,
        out_specs=[seg(0), so_spec],
        out_shape=[jax.ShapeDtypeStruct((n_seq * n_chunks * chunk, width), F32), so_shape],
        scratch_shapes=[pltpu.VMEM((heads, kd, vd), F32)],
        input_output_aliases=aliases,
        compiler_params=pltpu.CompilerParams(dimension_semantics=("parallel", "parallel", "arbitrary"),
                                             vmem_limit_bytes=VMEM_LIMIT),
        name="gla",
    )(*args, *more_args)


def _moe_kernel(be_ref, used_ref, xl_ref, xr_ref, wg_ref, wu_ref, wd_ref, o_ref):
    i = pl.program_id(0)

    @pl.when(used_ref[i] > 0)
    def _():
        x = jnp.concatenate([xl_ref[...], xr_ref[...]], axis=-1)
        hg = _bdot(x, wg_ref[0, 0])
        hu = _bdot(x, wu_ref[0, 0])
        hid = hg * jax.nn.sigmoid(hg) * hu
        o_ref[...] = _bdot(hid, wd_ref[0, 0])

    @pl.when(used_ref[i] == 0)
    def _():
        o_ref[...] = jnp.zeros_like(o_ref)


def moe_experts(block_e, block_used, x_left, x_right, w_gate, w_up, w_down, layer, *, rows):
    n_slots = x_left.shape[0]
    d, hid = w_gate.shape[-2:]
    grid_spec = pltpu.PrefetchScalarGridSpec(
        num_scalar_prefetch=2,
        grid=(n_slots // rows,),
        in_specs=[
            pl.BlockSpec((rows, d // 2), lambda i, be, used: (i, 0)),
            pl.BlockSpec((rows, d // 2), lambda i, be, used: (i, 0)),
            pl.BlockSpec((1, 1, d, hid), lambda i, be, used: (layer, be[i], 0, 0)),
            pl.BlockSpec((1, 1, d, hid), lambda i, be, used: (layer, be[i], 0, 0)),
            pl.BlockSpec((1, 1, hid, d), lambda i, be, used: (layer, be[i], 0, 0)),
        ],
        out_specs=pl.BlockSpec((rows, d), lambda i, be, used: (i, 0)),
    )
    return pl.pallas_call(
        _moe_kernel,
        grid_spec=grid_spec,
        out_shape=jax.ShapeDtypeStruct((n_slots, d), F32),
        compiler_params=pltpu.CompilerParams(dimension_semantics=("arbitrary",), vmem_limit_bytes=VMEM_LIMIT),
        name="moe_experts",
    )(block_e, block_used, x_left, x_right, w_gate, w_up, w_down)


def _first_max(p):
    idx = lax.broadcasted_iota(jnp.int32, p.shape, p.ndim - 1)
    top = jnp.max(p, axis=-1, keepdims=True)
    return top, jnp.min(jnp.where(p == top, idx, p.shape[-1]), axis=-1, keepdims=True)


def _prefix_counts(onehot):
    m, e = onehot.shape
    blk = LANES
    oh = onehot.reshape(m // blk, blk, e)
    r = jnp.arange(blk)
    inside = jnp.einsum('ts,bse->bte', (r[:, None] > r[None, :]).astype(F32), oh, precision=HIGHEST)
    totals = jnp.sum(oh, axis=1)
    b = jnp.arange(m // blk)
    before = jnp.einsum('ab,be->ae', (b[:, None] > b[None, :]).astype(F32), totals, precision=HIGHEST)
    return (inside + before[:, None, :]).reshape(m, e), jnp.sum(totals, axis=0)


def route_weights(wr_grp, wr_exp):
    d = wr_grp.shape[0]
    return jnp.concatenate([wr_grp, jnp.moveaxis(wr_exp, 0, 1).reshape(d, MOE_EXPERTS)], axis=1)


def hier_moe(logits, x_halves, br_grp, br_exp, w_gate, w_up, w_down, layer):
    t = logits.shape[0]
    grp_p, grp_i = _first_max(jax.nn.softmax(logits[:, :MOE_GROUPS] + br_grp, axis=-1))
    exp_all = logits[:, MOE_GROUPS:].reshape(t, MOE_GROUPS, MOE_PER_GROUP) + br_exp
    in_grp = lax.broadcasted_iota(jnp.int32, (t, MOE_GROUPS, 1), 1) == grp_i[:, :, None]
    exp_prob = jax.nn.softmax(jnp.sum(jnp.where(in_grp, exp_all, 0.0), axis=1), axis=-1)
    p1, i1 = _first_max(exp_prob)
    taken = lax.broadcasted_iota(jnp.int32, exp_prob.shape, 1) == i1
    p2, i2 = _first_max(jnp.where(taken, -jnp.inf, exp_prob))
    top_p = jnp.concatenate([p1, p2], axis=1)
    gates = grp_p * top_p / jnp.sum(top_p, -1, keepdims=True)
    expert = grp_i * MOE_PER_GROUP + jnp.concatenate([i1, i2], axis=1)
    m = t * MOE_TOPK
    onehot = (expert.reshape(m, 1) == jnp.arange(MOE_EXPERTS)[None, :]).astype(F32)
    ranks, counts = _prefix_counts(onehot)
    padded = jnp.ceil(counts / MOE_ROWS) * MOE_ROWS
    ends = jnp.sum(jnp.where(jnp.arange(MOE_EXPERTS)[:, None] <= jnp.arange(MOE_EXPERTS)[None, :],
                             padded[:, None], 0.0), axis=0)
    slot = jnp.sum(onehot * (ranks + (ends - padded)[None, :]), axis=1).astype(jnp.int32)
    n_blocks = -(-m // MOE_ROWS) + MOE_EXPERTS
    n_slots = n_blocks * MOE_ROWS
    flat_t = lax.broadcasted_iota(jnp.int32, (t, MOE_TOPK), 0).reshape(m)
    slot_tok = (jnp.arange(n_slots, dtype=jnp.int32) % t).at[slot].set(flat_t)
    block_start = jnp.arange(n_blocks, dtype=jnp.int32) * MOE_ROWS
    ends_i = ends.astype(jnp.int32)
    block_e = jnp.minimum(jnp.sum((block_start[:, None] >= ends_i[None, :]).astype(jnp.int32), axis=1),
                          MOE_EXPERTS - 1)
    block_used = (block_start < ends_i[-1]).astype(jnp.int32)
    ys = moe_experts(block_e, block_used, x_halves[0][slot_tok], x_halves[1][slot_tok], w_gate, w_up, w_down, layer,
                     rows=MOE_ROWS)
    slot = slot.reshape(t, MOE_TOPK)
    return ys[slot[:, 0]], ys[slot[:, 1]], gates


def _ln_kernel(*refs, n_terms, gated, extra):
    x_ref = refs[0]
    terms = refs[1:1 + n_terms]
    pos = 1 + n_terms
    gate_ref = refs[pos] if gated else None
    pos += int(gated)
    g_ref, b_ref = refs[pos:pos + 2]
    pos += 2
    route_ref = refs[pos] if extra == "moe" else None
    outs = refs[pos + int(extra == "moe"):]
    h = DN_ALPHA * x_ref[...]
    for i, term in enumerate(terms):
        h = h + (term[...] * gate_ref[:, i:i + 1] if gated else term[...])
    mu = jnp.mean(h, -1, keepdims=True)
    dev = h - mu
    var = jnp.mean(jnp.square(dev), -1, keepdims=True)
    y = dev * lax.rsqrt(var + NORM_EPS) * g_ref[...] + b_ref[...]
    outs[0][...] = y
    if extra == "bf16":
        outs[1][...] = y.astype(BF16)
    elif extra == "moe":
        half = y.shape[-1] // 2
        outs[1][...] = y[:, :half]
        outs[2][...] = y[:, half:]
        outs[3][...] = _hdot(y, route_ref[...])


def residual_layer_norm(x, terms, gates, g, b, *, extra, w_route=None):
    t, d = x.shape
    rows = pl.BlockSpec((LN_ROWS, d), lambda i: (i, 0))
    vec = pl.BlockSpec((1, d), lambda i: (0, 0))
    gate_specs = [] if gates is None else [pl.BlockSpec((LN_ROWS, gates.shape[1]), lambda i: (i, 0))]
    gate_args = [] if gates is None else [gates]
    out_shape = [jax.ShapeDtypeStruct((t, d), F32)]
    out_specs = [rows]
    if extra == "bf16":
        out_shape.append(jax.ShapeDtypeStruct((t, d), BF16))
        out_specs.append(rows)
    route_specs, route_args = [], []
    if extra == "moe":
        n_route = w_route.shape[1]
        out_shape += [jax.ShapeDtypeStruct((t, d // 2), F32)] * 2 + [jax.ShapeDtypeStruct((t, n_route), F32)]
        out_specs += [pl.BlockSpec((LN_ROWS, d // 2), lambda i: (i, 0))] * 2
        out_specs += [pl.BlockSpec((LN_ROWS, n_route), lambda i: (i, 0))]
        route_specs, route_args = [pl.BlockSpec((d, n_route), lambda i: (0, 0))], [w_route]
    return pl.pallas_call(
        functools.partial(_ln_kernel, n_terms=len(terms), gated=gates is not None, extra=extra),
        grid=(t // LN_ROWS,),
        in_specs=[rows] * (1 + len(terms)) + gate_specs + [vec, vec] + route_specs,
        out_specs=out_specs,
        out_shape=out_shape,
        compiler_params=pltpu.CompilerParams(dimension_semantics=("parallel",), vmem_limit_bytes=VMEM_LIMIT),
        name="residual_layer_norm",
    )(x, *terms, *gate_args, g.reshape(1, d), b.reshape(1, d), *route_args)


def _pad_len(t, length):
    return jnp.pad(t, [(0, 0), (0, length - t.shape[1])] + [(0, 0)] * (t.ndim - 2))


def _shift_rows(p, prev, b, l):
    seq = p.reshape(b, l, p.shape[-1])
    return jnp.concatenate([prev[:, None, :], seq[:, :-1]], axis=1).reshape(p.shape), seq[:, -1]


def even_mixer(x, lens, e, shift0, wkv0, conv0, ssm0, v_first, vres, w_in, w_out, mu, w0, w_up, a0, a_up, g_up,
               k_k, k_a, r_k, lnx_g, lnx_b, conv_w, conv_b, dt_bias, a_log, d_skip, norm_w):
    (bp, lp, _), (bs, ls, _) = lens
    np_rows = bp * lp
    t = x.shape[0]
    p = mm(x, w_in, e)
    p_rw, p_ssm = p[:, :RW_PROJ], p[:, RW_PROJ:]
    prev_p, shift_p = _shift_rows(p_rw[:np_rows], shift0[0], bp, lp)
    prev_s, shift_s = _shift_rows(p_rw[np_rows:], shift0[1], bs, ls)
    pm = p_rw + (jnp.concatenate([prev_p, prev_s], axis=0) - p_rw) * mu
    cuts = [RW_WIDTH, 2 * RW_WIDTH, 3 * RW_WIDTH, 3 * RW_WIDTH + RW_LORA_W, 3 * RW_WIDTH + RW_LORA_W + RW_LORA_A]
    r, k, v, wd, ad, gd = jnp.split(pm, cuts, axis=-1)
    log_w = -jax.nn.softplus(-(w0 + mm(jnp.tanh(wd), w_up))) - 0.5
    lw = -jnp.exp(log_w)
    if vres is None:
        v_first = v
    else:
        v0, v_down, v_up = vres
        v = v + (v_first - v) * jax.nn.sigmoid(v0 + mm(mm(v, v_down), v_up))
    a = jax.nn.sigmoid(a0 + mm(ad, a_up))
    g = mm(jax.nn.sigmoid(gd), g_up)
    heads = lambda u: u.reshape(t, RW_HEADS, RW_HEAD)
    kk = heads(k * k_k)
    kk = kk * lax.rsqrt(jnp.maximum(jnp.sum(jnp.square(kk), -1, keepdims=True), 1e-24))
    k = k * (1.0 + (a - 1.0) * k_a)
    seqs = [r, lw, k, v, kk.reshape(t, RW_WIDTH), a, g]
    chunk_p = min(WKV_CHUNK, lp)
    o_p, wkv_p = wkv7(*seqs, r_k, lnx_g, lnx_b, *wkv0[0], n_seq=bp, chunk=chunk_p, n_chunks=lp // chunk_p,
                      pairs=RW_HEADS // 2)
    padded = [_pad_len(u[np_rows:].reshape(bs, ls, RW_WIDTH), SAMPLE_PAD).reshape(bs * SAMPLE_PAD, RW_WIDTH)
              for u in seqs]
    o_s, wkv_s = wkv7(*padded, r_k, lnx_g, lnx_b, *wkv0[1], n_seq=bs, chunk=SAMPLE_PAD, n_chunks=1,
                      pairs=RW_HEADS // 2)
    o_rw = jnp.concatenate([o_p, o_s.reshape(bs, SAMPLE_PAD, RW_WIDTH)[:, :ls].reshape(bs * ls, RW_WIDTH)], axis=0)
    z = p_ssm[:, :SSM_WIDTH]
    xbc_raw = p_ssm[:, SSM_WIDTH:SSM_WIDTH + SSM_CONV_CH]
    dt_raw = p_ssm[:, SSM_WIDTH + SSM_CONV_CH:]

    def conv_io(u, buf, b, l):
        full = jnp.concatenate([buf, u.reshape(b, l, SSM_CONV_CH)[:, max(l - (SSM_CONV - 1), 0):]], axis=1)
        return jnp.pad(buf, [(0, 0), (CONV_TAIL - (SSM_CONV - 1), 0), (0, 0)]), full[:, full.shape[1] - (SSM_CONV - 1):]

    prev_p, conv_p = conv_io(xbc_raw[:np_rows], conv0[0], bp, lp)
    prev_s, conv_s = conv_io(xbc_raw[np_rows:], conv0[1], bs, ls)
    dt = jax.nn.softplus(dt_raw + dt_bias)
    seqs = [xbc_raw, z, dt, dt * -jnp.exp(a_log)]
    chunk_p = min(SSD_CHUNK, lp)
    y_p, ssm_p = ssd(seqs[0], prev_p, conv_w, conv_b, *seqs[1:], d_skip, norm_w, *ssm0[0], n_seq=bp, chunk=chunk_p,
                     n_chunks=lp // chunk_p)
    padded = [_pad_len(u[np_rows:].reshape(bs, ls, u.shape[-1]), SAMPLE_PAD).reshape(bs * SAMPLE_PAD, u.shape[-1])
              for u in seqs]
    y_s, ssm_s = ssd(padded[0], prev_s, conv_w, conv_b, *padded[1:], d_skip, norm_w, *ssm0[1], n_seq=bs,
                     chunk=SAMPLE_PAD, n_chunks=1)
    y = jnp.concatenate([y_p, y_s.reshape(bs, SAMPLE_PAD, SSM_WIDTH)[:, :ls].reshape(bs * ls, SSM_WIDTH)], axis=0)
    mix = mm(jnp.concatenate([o_rw, y], axis=-1).astype(BF16), w_out, e, tm=MM_TM // 2)
    return mix, v_first, (shift_p, shift_s), (wkv_p, wkv_s), (conv_p, conv_s), (ssm_p, ssm_s)


def hgrn2_mixer(x, lens, o, s0, w_in, w_out, lower, norm_w):
    (bp, lp, _), (bs, ls, _) = lens
    p = mm(x, w_in, o)
    y_p, hg_p = gla(p, lower, norm_w, *s0[0], n_seq=bp, chunk=GLA_CHUNK, n_chunks=lp // GLA_CHUNK,
                    heads=GLA_HEADS_PER_STEP, valid=GLA_CHUNK)
    p_s = _pad_len(p[bp * lp:].reshape(bs, ls, p.shape[-1]), SAMPLE_PAD).reshape(bs * SAMPLE_PAD, p.shape[-1])
    y_s, hg_s = gla(p_s, lower, norm_w, *s0[1], n_seq=bs, chunk=SAMPLE_PAD, n_chunks=1, heads=HG_HEADS, valid=ls)
    y_s = y_s.reshape(bs, SAMPLE_PAD, -1)[:, :ls].reshape(bs * ls, -1)
    return mm(jnp.concatenate([y_p, y_s], axis=0).astype(BF16), w_out, o), (hg_p, hg_s)


def kernel(x_prompt, x_sample, state_rwkv_shift, state_rwkv_wkv, state_ssm_conv, state_ssm, state_hgrn,
           ev_w_in, ev_w_out, rw_mu, rw_w0, rw_w_up, rw_a0, rw_a_up, rw_g_up, rw_k_k, rw_k_a, rw_r_k,
           rw_lnx_g, rw_lnx_b, rw_v0, rw_v_down, rw_v_up, ssm_conv_w, ssm_conv_b, ssm_dt_bias, ssm_a_log,
           ssm_d, ssm_norm_w, od_w_in, od_w_out, hg_lower_bounds, hg_norm_w, ln1_g, ln1_b, ln2_g, ln2_b,
           moe_wr_grp, moe_br_grp, moe_wr_exp, moe_br_exp, moe_w_gate, moe_w_up, moe_w_down):
    lb_soft = jax.nn.softmax(hg_lower_bounds, axis=0)
    lower = jnp.cumsum(lb_soft, axis=0) - lb_soft[0]
    bp, lp, d = x_prompt.shape
    bs, ls, _ = x_sample.shape
    lens = ((bp, lp, None), (bs, ls, None))
    x = jnp.concatenate([x_prompt.reshape(bp * lp, d), x_sample.reshape(bs * ls, d)], axis=0)
    zeros_p = lambda s: jnp.zeros((bp,) + s.shape[2:], s.dtype)
    x_bf16 = x.astype(BF16)
    v_first = None
    shifts, wkvs, convs, ssms, hgs = [], [], [], [], []
    wkv_all = ssm_all = hg_all = None
    for layer in range(DEPTH):
        if layer % 2 == 0:
            e = layer // 2
            n_even = state_rwkv_wkv.shape[0]
            vres = None if e == 0 else (rw_v0[e - 1], rw_v_down[e - 1], rw_v_up[e - 1])
            mix, v_first, s_sh, s_wkv, s_cv, s_ss = even_mixer(
                x_bf16, lens, e, (zeros_p(state_rwkv_shift), state_rwkv_shift[e]),
                ((zeros_p(state_rwkv_wkv), None), (state_rwkv_wkv, e, (n_even, e, wkv_all))),
                (zeros_p(state_ssm_conv), state_ssm_conv[e]),
                ((zeros_p(state_ssm), None), (state_ssm, e, (n_even, e, ssm_all))), v_first, vres,
                ev_w_in, ev_w_out, rw_mu[e], rw_w0[e], rw_w_up[e], rw_a0[e], rw_a_up[e], rw_g_up[e], rw_k_k[e],
                rw_k_a[e], rw_r_k[e], rw_lnx_g[e], rw_lnx_b[e], ssm_conv_w[e], ssm_conv_b[e], ssm_dt_bias[e],
                ssm_a_log[e], ssm_d[e], ssm_norm_w[e])
            shifts.append(s_sh)
            convs.append(s_cv)
            wkvs.append(s_wkv[0])
            ssms.append(s_ss[0])
            wkv_all, ssm_all = s_wkv[1], s_ss[1]
        else:
            o = layer // 2
            mix, s_hg = hgrn2_mixer(x_bf16, lens, o,
                                    ((zeros_p(state_hgrn), None), (state_hgrn, o, (state_hgrn.shape[0], o, hg_all))),
                                    od_w_in, od_w_out, lower[layer], hg_norm_w[o])
            hgs.append(s_hg[0])
            hg_all = s_hg[1]
        x, x_lo, x_hi, logits = residual_layer_norm(x, [mix], None, ln1_g[layer], ln1_b[layer], extra="moe",
                                                    w_route=route_weights(moe_wr_grp[layer], moe_wr_exp[layer]))
        y0, y1, gates = hier_moe(logits, (x_lo, x_hi), moe_br_grp[layer], moe_br_exp[layer],
                                 moe_w_gate, moe_w_up, moe_w_down, layer)
        x, x_bf16 = residual_layer_norm(x, [y0, y1], gates, ln2_g[layer], ln2_b[layer], extra="bf16")
    stack = lambda pairs, j: jnp.stack([p[j] for p in pairs])
    np_rows = bp * lp
    return (x[:np_rows].reshape(bp, lp, d), x[np_rows:].reshape(bs, ls, d),
            stack(shifts, 0), jnp.stack(wkvs), stack(convs, 0), jnp.stack(ssms), jnp.stack(hgs),
            stack(shifts, 1), wkv_all, stack(convs, 1), ssm_all, hg_all)
```
